```python
import math
import jax, jax.numpy as jnp
from jax import lax
import numpy as np

D_MODEL = 1024
BATCH = 2
SEQ = 8192
DEPTH = 2

HEAD_DIM = 64
D_MIX = D_MODEL
GROUP_W = D_MIX // 4
N_HEADS_A = GROUP_W // HEAD_DIM
N_HEADS_B = GROUP_W // HEAD_DIM
N_GROUPS_C = 4
N_GROUPS_D = 4
MOBA_BLOCK = 256
MOBA_TOPK = 3
QBLOCK = 128
ROPE_THETA = 500000.0
ROPE_DIM = HEAD_DIM // 4
SGU_CHUNK = 128
POOL_WINDOWS = (2, 4, 8, 16)
D_FF = 2816
CONV_WIDTH = 3
DN_ALPHA = (2 * DEPTH) ** 0.25
DN_BETA = (8 * DEPTH) ** -0.25
LN_EPS = 1e-5
IN_SIZES = (GROUP_W, GROUP_W, GROUP_W, GROUP_W, GROUP_W, GROUP_W, N_HEADS_B, GROUP_W, GROUP_W, GROUP_W)
IN_SPLITS = tuple(int(s) for s in np.cumsum(IN_SIZES)[:-1])
N_IN = int(sum(IN_SIZES))

kernel_name = "hybrid_moba_fox_sgu_pool_block"


def layer_norm(x, g, b):
    xf = x.astype(jnp.float32)
    mu = jnp.mean(xf, axis=-1, keepdims=True)
    var = jnp.mean(jnp.square(xf - mu), axis=-1, keepdims=True)
    return ((xf - mu) * lax.rsqrt(var + LN_EPS) * g.astype(jnp.float32) + b.astype(jnp.float32)).astype(x.dtype)


def to_heads(t, n_heads):
    b, s, _ = t.shape
    return t.reshape(b, s, n_heads, -1).transpose(0, 2, 1, 3)


def from_heads(t):
    b, h, s, d = t.shape
    return t.transpose(0, 2, 1, 3).reshape(b, s, h * d)


def rotary_tables(seq_len):
    pos = jnp.arange(seq_len, dtype=jnp.float32)
    inv_freq = ROPE_THETA ** (-jnp.arange(0, ROPE_DIM, 2, dtype=jnp.float32) / ROPE_DIM)
    ang = pos[:, None] * inv_freq[None, :]
    return jnp.cos(ang), jnp.sin(ang)


def partial_rotary(x, cos, sin):
    half = ROPE_DIM // 2
    cos = cos.astype(x.dtype)
    sin = sin.astype(x.dtype)
    x1 = x[..., :half]
    x2 = x[..., half:ROPE_DIM]
    return jnp.concatenate([x1 * cos - x2 * sin, x1 * sin + x2 * cos, x[..., ROPE_DIM:]], axis=-1)


def moba_attention(q, k, v):
    bsz, h, s, hd = q.shape
    s_pad = -(-s // MOBA_BLOCK) * MOBA_BLOCK
    pad = ((0, 0), (0, 0), (0, s_pad - s), (0, 0))
    q, k, v = jnp.pad(q, pad), jnp.pad(k, pad), jnp.pad(v, pad)
    nb = s_pad // MOBA_BLOCK
    n_sel = min(MOBA_TOPK, nb)
    scale = hd ** -0.5
    kb = k.reshape(bsz, h, nb, MOBA_BLOCK, hd)
    vb = v.reshape(bsz, h, nb, MOBA_BLOCK, hd)
    kbar = jnp.mean(kb.astype(jnp.float32), axis=3)
    gate = jnp.einsum('bhsd,bhnd->bhsn', q.astype(jnp.float32), kbar)
    q_blk = jnp.arange(s_pad) // MOBA_BLOCK
    fully_past = jnp.arange(nb)[None, :] < q_blk[:, None]
    gate = jnp.where(fully_past, gate, -jnp.inf)
    _, sel = lax.top_k(gate, n_sel)
    bi = jnp.arange(bsz)[:, None, None, None]
    hi = jnp.arange(h)[None, :, None, None]

    def one_chunk(n):
        start = n * QBLOCK
        blk = start // MOBA_BLOCK
        qc = lax.dynamic_slice_in_dim(q, start, QBLOCK, axis=2)
        selc = lax.dynamic_slice_in_dim(sel, start, QBLOCK, axis=2)
        valid = selc < blk
        ks = kb[bi, hi, selc]
        vs = vb[bi, hi, selc]
        s_sel = jnp.einsum('bhqd,bhqnjd->bhqnj', qc, ks).astype(jnp.float32) * scale
        s_sel = jnp.where(valid[..., None], s_sel, -jnp.inf).reshape(bsz, h, QBLOCK, n_sel * MOBA_BLOCK)
        k_own = lax.dynamic_slice_in_dim(k, blk * MOBA_BLOCK, MOBA_BLOCK, axis=2)
        v_own = lax.dynamic_slice_in_dim(v, blk * MOBA_BLOCK, MOBA_BLOCK, axis=2)
        s_own = jnp.einsum('bhqd,bhjd->bhqj', qc, k_own).astype(jnp.float32) * scale
        qpos = start + jnp.arange(QBLOCK)
        kpos = blk * MOBA_BLOCK + jnp.arange(MOBA_BLOCK)
        s_own = jnp.where(kpos[None, :] <= qpos[:, None], s_own, -jnp.inf)
        p = jax.nn.softmax(jnp.concatenate([s_sel, s_own], axis=-1), axis=-1)
        p_sel = p[..., :n_sel * MOBA_BLOCK].reshape(bsz, h, QBLOCK, n_sel, MOBA_BLOCK).astype(v.dtype)
        p_own = p[..., n_sel * MOBA_BLOCK:].astype(v.dtype)
        return (jnp.einsum('bhqnj,bhqnjd->bhqd', p_sel, vs)
                + jnp.einsum('bhqj,bhjd->bhqd', p_own, v_own))

    outs = lax.map(one_chunk, jnp.arange(s_pad // QBLOCK))
    out = outs.transpose(1, 2, 0, 3, 4).reshape(bsz, h, s_pad, hd)
    return out[:, :, :s]


def forgetting_attention(q, k, v, log_f):
    bsz, h, s, hd = q.shape
    scale = hd ** -0.5
    c = jnp.cumsum(log_f, axis=-1)
    kpos = jnp.arange(s)

    def one_block(n):
        start = n * QBLOCK
        qc = lax.dynamic_slice_in_dim(q, start, QBLOCK, axis=2)
        cq = lax.dynamic_slice_in_dim(c, start, QBLOCK, axis=2)
        logits = (jnp.einsum('bhqd,bhkd->bhqk', qc, k).astype(jnp.float32) * scale
                  + cq[..., :, None] - c[..., None, :])
        qpos = start + jnp.arange(QBLOCK)
        logits = jnp.where(kpos[None, :] <= qpos[:, None], logits, -jnp.inf)
        p = jax.nn.softmax(logits, axis=-1).astype(v.dtype)
        return jnp.einsum('bhqk,bhkd->bhqd', p, v)

    outs = lax.map(one_block, jnp.arange(s // QBLOCK))
    return outs.transpose(1, 2, 0, 3, 4).reshape(bsz, h, s, hd)


def causal_spatial_gating(u, v, ln_g, ln_b, w_s, b_s):
    bsz, s, width = v.shape
    g = N_GROUPS_C
    c = width // g
    vn = layer_norm(v.reshape(bsz, s, g, c), ln_g.reshape(g, c), ln_b.reshape(g, c))
    vc = vn.reshape(bsz, s // SGU_CHUNK, SGU_CHUNK, g, c)
    causal = jnp.tril(jnp.ones((SGU_CHUNK, SGU_CHUNK), dtype=bool))
    w = jnp.where(causal[None], w_s, jnp.zeros_like(w_s))
    mixed = jnp.einsum('gts,bnsgc->bntgc', w, vc) + b_s.T[None, None, :, :, None]
    return u * mixed.reshape(bsz, s, width)


def multiscale_pool(p, w_g, scale):
    bsz, s, width = p.shape
    g = len(POOL_WINDOWS)
    c = width // g
    pg = p.reshape(bsz, s, g, c).astype(jnp.float32)
    cs = jnp.concatenate([jnp.zeros((bsz, 1, g, c), jnp.float32), jnp.cumsum(pg, axis=1)], axis=1)
    t = jnp.arange(s)
    outs = []
    for gi, win in enumerate(POOL_WINDOWS):
        cs_g = cs[:, :, gi]
        lo = jnp.maximum(t + 1 - win, 0)
        window_sum = cs_g[:, 1:] - cs_g[:, lo]
        count = jnp.minimum(t + 1, win).astype(jnp.float32)
        outs.append(window_sum / count[None, :, None] - pg[:, :, gi])
    pooled = jnp.stack(outs, axis=2).astype(p.dtype)
    mixed = jnp.einsum('bsgc,gcd->bsgd', pooled, w_g)
    return mixed.reshape(bsz, s, width) * scale


def hybrid_mixer(x, w_in, b_forget, sgu_ln_g, sgu_ln_b, sgu_w, sgu_b, pool_w, pool_scale, w_o, cos, sin):
    z = x @ w_in
    aq, ak, av, bq, bk, bv, bf, cu, cv, dp = jnp.split(z, IN_SPLITS, axis=-1)
    ya = moba_attention(partial_rotary(to_heads(aq, N_HEADS_A), cos, sin),
                        partial_rotary(to_heads(ak, N_HEADS_A), cos, sin),
                        to_heads(av, N_HEADS_A))
    log_f = jax.nn.log_sigmoid((bf + b_forget).astype(jnp.float32)).transpose(0, 2, 1)
    yb = forgetting_attention(to_heads(bq, N_HEADS_B), to_heads(bk, N_HEADS_B),
                              to_heads(bv, N_HEADS_B), log_f)
    yc = causal_spatial_gating(jax.nn.gelu(cu, approximate=False), jax.nn.gelu(cv, approximate=False),
                               sgu_ln_g, sgu_ln_b, sgu_w, sgu_b)
    yd = multiscale_pool(dp, pool_w, pool_scale)
    y = jnp.concatenate([from_heads(ya), from_heads(yb), yc, yd], axis=-1)
    return y @ w_o


def conv_gated_ffn(x, w_up, conv_w, conv_b, w_down):
    h = x @ w_up
    h = lax.conv_general_dilated(h, conv_w[:, None, :], window_strides=(1,),
                                 padding=[(CONV_WIDTH - 1, 0)],
                                 dimension_numbers=('NWC', 'WIO', 'NWC'),
                                 feature_group_count=h.shape[-1]) + conv_b
    gate, val = jnp.split(h, 2, axis=-1)
    return (jax.nn.silu(gate) * val) @ w_down


def setup_inputs(seed: int = 0) -> dict:
    key = jax.random.key(seed)
    ks = jax.random.split(key, 20)
    L = DEPTH
    nrm = jax.random.normal
    f32 = jnp.float32
    ch = GROUP_W // N_GROUPS_D
    return {
        'x': nrm(ks[0], (BATCH, SEQ, D_MODEL), f32),
        'w_in': nrm(ks[1], (L, D_MODEL, N_IN), f32) * D_MODEL ** -0.5,
        'b_forget': 3.0 + 0.1 * nrm(ks[2], (L, N_HEADS_B), f32),
        'sgu_ln_g': 1.0 + 0.05 * nrm(ks[3], (L, GROUP_W), f32),
        'sgu_ln_b': 0.02 * nrm(ks[4], (L, GROUP_W), f32),
        'sgu_w': nrm(ks[5], (L, N_GROUPS_C, SGU_CHUNK, SGU_CHUNK), f32) * SGU_CHUNK ** -0.5,
        'sgu_b': 1.0 + 0.1 * nrm(ks[6], (L, N_GROUPS_C, SGU_CHUNK), f32),
        'pool_w': nrm(ks[7], (L, N_GROUPS_D, ch, ch), f32) * ch ** -0.5,
        'pool_scale': 1.0 + 0.1 * nrm(ks[8], (L, GROUP_W), f32),
        'w_o': nrm(ks[9], (L, D_MIX, D_MODEL), f32) * D_MIX ** -0.5 * DN_BETA,
        'ln1_g': 1.0 + 0.05 * nrm(ks[10], (L, D_MODEL), f32),
        'ln1_b': 0.02 * nrm(ks[11], (L, D_MODEL), f32),
        'w_up': nrm(ks[12], (L, D_MODEL, 2 * D_FF), f32) * D_MODEL ** -0.5,
        'conv_w': nrm(ks[13], (L, CONV_WIDTH, 2 * D_FF), f32) * CONV_WIDTH ** -0.5,
        'conv_b': 0.02 * nrm(ks[14], (L, 2 * D_FF), f32),
        'w_down': nrm(ks[15], (L, D_FF, D_MODEL), f32) * D_FF ** -0.5 * DN_BETA,
        'ln2_g': 1.0 + 0.05 * nrm(ks[16], (L, D_MODEL), f32),
        'ln2_b': 0.02 * nrm(ks[17], (L, D_MODEL), f32),
    }


def reference(x, w_in, b_forget, sgu_ln_g, sgu_ln_b, sgu_w, sgu_b, pool_w, pool_scale, w_o,
              ln1_g, ln1_b, w_up, conv_w, conv_b, w_down, ln2_g, ln2_b):
    cos, sin = rotary_tables(x.shape[1])
    for l in range(DEPTH):
        y = hybrid_mixer(x, w_in[l], b_forget[l], sgu_ln_g[l], sgu_ln_b[l], sgu_w[l], sgu_b[l],
                         pool_w[l], pool_scale[l], w_o[l], cos, sin)
        x = layer_norm(DN_ALPHA * x + y, ln1_g[l], ln1_b[l])
        y = conv_gated_ffn(x, w_up[l], conv_w[l], conv_b[l], w_down[l])
        x = layer_norm(DN_ALPHA * x + y, ln2_g[l], ln2_b[l])
    return x
```

```python
import functools

import numpy as np
import jax
import jax.numpy as jnp
from jax import lax
from jax.experimental import pallas as pl
from jax.experimental.pallas import tpu as pltpu

F32 = jnp.float32
BF16 = jnp.bfloat16

D_MODEL = 1024
DEPTH = 2
HEAD_DIM = 64
N_HEADS = 4
GROUP_W = 256
HEAD_PAD = 128
MOBA_BLOCK = 256
MOBA_TOPK = 3
ROPE_THETA = 500000.0
ROPE_DIM = HEAD_DIM // 4
SGU_CHUNK = 128
SGU_GROUPS = 4
POOL_WINDOWS = (2, 4, 8, 16)
POOL_HALO = 16
D_FF = 2816
DN_ALPHA = (2 * DEPTH) ** 0.25
LN_EPS = 1e-5
ATTN_SCALE = HEAD_DIM ** -0.5
MASK_BIAS = -1e30

FRONT_ROWS = MOBA_BLOCK
ATTN_TQ = 256
ATTN_TK = 256
PROJ_ROWS = 512
FFN_ROWS = 512
FFN_COLS = 256
FFN_HALO = 16
VMEM_LIMIT = 56 * 1024 * 1024

_QKV_W = N_HEADS * HEAD_PAD
COL_A = 0
COL_B = 3 * _QKV_W
COL_REST = 6 * _QKV_W
N_COLS = COL_REST + 3 * GROUP_W + 128


def _layer_norm_rows(r, g, b):
    mu = jnp.mean(r, axis=-1, keepdims=True)
    d = r - mu
    var = jnp.mean(d * d, axis=-1, keepdims=True)
    return d * lax.rsqrt(var + LN_EPS) * g + b


def _split_bf16(t):
    hi = t.astype(BF16)
    lo = (t - hi.astype(F32)).astype(BF16)
    return hi, lo


def _group_mean(t, avg):
    hi, lo = _split_bf16(t)
    return (jnp.dot(hi, avg, preferred_element_type=F32) + jnp.dot(lo, avg, preferred_element_type=F32))


def _front_kernel(x_ref, w_ref, cos_ref, sina_ref, sinb_ref, bfg_ref, lng_ref, lnb_ref, sw_ref, sb_ref,
                  avg_ref, pw_ref, ps_ref,
                  qa_ref, ka_ref, va_ref, qb_ref, kb_ref, vb_ref, yc_ref, yd_ref,
                  kbar_ref, carry_ref, halo_ref):
    rows = FRONT_ROWS
    i = pl.program_id(1)
    n_blk = kbar_ref.shape[1]

    @pl.when(i == 0)
    def _():
        kbar_ref[...] = jnp.zeros_like(kbar_ref)
        carry_ref[...] = jnp.zeros_like(carry_ref)
        halo_ref[...] = jnp.zeros_like(halo_ref)

    xb = x_ref[0].astype(BF16)
    lane = lax.broadcasted_iota(jnp.int32, (rows, HEAD_PAD), 1)

    za = jnp.dot(xb, w_ref[:, COL_A:COL_A + 3 * _QKV_W], preferred_element_type=F32)
    cos_t, sin_a, sin_b = cos_ref[...], sina_ref[...], sinb_ref[...]

    def rotary(t):
        return (t * cos_t + pltpu.roll(t, HEAD_PAD - ROPE_DIM // 2, 1) * sin_a
                + pltpu.roll(t, ROPE_DIM // 2, 1) * sin_b)

    blk_row = lax.broadcasted_iota(jnp.int32, (n_blk, rows), 0)
    for h in range(N_HEADS):
        q = rotary(za[:, h * HEAD_PAD:(h + 1) * HEAD_PAD])
        k = rotary(za[:, _QKV_W + h * HEAD_PAD:_QKV_W + (h + 1) * HEAD_PAD])
        v = za[:, 2 * _QKV_W + h * HEAD_PAD:2 * _QKV_W + (h + 1) * HEAD_PAD]
        gate_t = lax.dot_general(kbar_ref[h], q, (((1,), (1,)), ((), ())),
                                 precision=lax.Precision.HIGHEST, preferred_element_type=F32)
        g = jnp.where(blk_row < i, gate_t, -jnp.inf)
        chosen = jnp.zeros((n_blk, rows), F32)
        for _ in range(MOBA_TOPK):
            top = jnp.max(g, axis=0, keepdims=True)
            first = jnp.min(jnp.where(g == top, blk_row, n_blk), axis=0, keepdims=True)
            pick = blk_row == first
            chosen = jnp.where(pick, 1.0, chosen)
            g = jnp.where(pick, -jnp.inf, g)
        keep = jnp.where(blk_row < i, chosen, jnp.where(blk_row == i, 1.0, 0.0))
        bias_t = jnp.where(keep > 0.0, 0.0, MASK_BIAS)
        bias_t = jnp.concatenate([jnp.zeros((HEAD_DIM, rows), F32), bias_t,
                                  jnp.zeros((HEAD_PAD - HEAD_DIM - n_blk, rows), F32)], axis=0)
        qa_ref[0, h] = (q * ATTN_SCALE + bias_t.T).astype(BF16)
        ka_ref[0, h] = jnp.where(lane == HEAD_DIM + i, 1.0, k).astype(BF16)
        va_ref[0, h] = jnp.where(lane == HEAD_DIM, 1.0, v).astype(BF16)
        kbar_ref[h, pl.ds(i, 1), :] = jnp.sum(k, axis=0, keepdims=True) * (1.0 / MOBA_BLOCK)

    zr = jnp.dot(xb, w_ref[:, COL_REST:N_COLS], preferred_element_type=F32)
    cu = zr[:, 0:GROUP_W]
    cv = zr[:, GROUP_W:2 * GROUP_W]
    dp = zr[:, 2 * GROUP_W:3 * GROUP_W]
    fl = zr[:, 3 * GROUP_W:3 * GROUP_W + 128] + bfg_ref[...]
    log_f = jnp.minimum(fl, 0.0) - jnp.log1p(jnp.exp(-jnp.abs(fl)))
    log_f = jnp.where(lane < N_HEADS, log_f, 0.0)
    r_i = lax.broadcasted_iota(jnp.int32, (rows, rows), 0)
    c_i = lax.broadcasted_iota(jnp.int32, (rows, rows), 1)
    tri = jnp.where(c_i <= r_i, 1.0, 0.0)
    cum = jnp.dot(tri, log_f, precision=lax.Precision.HIGHEST, preferred_element_type=F32) + carry_ref[0:1, :]
    carry_ref[0:1, :] = cum[rows - 1:rows, :]

    zb = jnp.dot(xb, w_ref[:, COL_B:COL_B + 3 * _QKV_W], preferred_element_type=F32)
    for h in range(N_HEADS):
        q = zb[:, h * HEAD_PAD:(h + 1) * HEAD_PAD]
        k = zb[:, _QKV_W + h * HEAD_PAD:_QKV_W + (h + 1) * HEAD_PAD]
        v = zb[:, 2 * _QKV_W + h * HEAD_PAD:2 * _QKV_W + (h + 1) * HEAD_PAD]
        c = jnp.broadcast_to(cum[:, h:h + 1], (rows, HEAD_PAD))
        hi = c.astype(BF16).astype(F32)
        mid = (c - hi).astype(BF16).astype(F32)
        lo = c - hi - mid
        q_aug = jnp.where(lane < HEAD_DIM, q * ATTN_SCALE,
                jnp.where(lane == HEAD_DIM, hi,
                jnp.where(lane == HEAD_DIM + 1, mid,
                jnp.where(lane == HEAD_DIM + 2, lo,
                jnp.where(lane < HEAD_DIM + 6, 1.0, 0.0)))))
        k_aug = jnp.where(lane < HEAD_DIM, k,
                jnp.where(lane < HEAD_DIM + 3, 1.0,
                jnp.where(lane == HEAD_DIM + 3, -hi,
                jnp.where(lane == HEAD_DIM + 4, -mid,
                jnp.where(lane == HEAD_DIM + 5, -lo, 0.0)))))
        qb_ref[0, h] = q_aug.astype(BF16)
        kb_ref[0, h] = k_aug.astype(BF16)
        vb_ref[0, h] = jnp.where(lane == HEAD_DIM, 1.0, v).astype(BF16)

    inv_sqrt2 = np.float32(1.0 / np.sqrt(2.0))
    u = 0.5 * cu * (1.0 + lax.erf(cu * inv_sqrt2))
    vg = 0.5 * cv * (1.0 + lax.erf(cv * inv_sqrt2))
    avg = avg_ref[...]
    mu = _group_mean(vg, avg)
    dv = vg - mu
    var = _group_mean(dv * dv, avg)
    vn = dv * lax.rsqrt(var + LN_EPS) * lng_ref[...] + lnb_ref[...]
    lane_w = lax.broadcasted_iota(jnp.int32, (SGU_CHUNK, GROUP_W), 1)
    t_i = lax.broadcasted_iota(jnp.int32, (SGU_CHUNK, SGU_CHUNK), 0)
    s_i = lax.broadcasted_iota(jnp.int32, (SGU_CHUNK, SGU_CHUNK), 1)
    w_tril = [jnp.where(s_i <= t_i, sw_ref[g], 0.0).astype(BF16) for g in range(SGU_GROUPS)]
    for c in range(rows // SGU_CHUNK):
        vn_c = vn[c * SGU_CHUNK:(c + 1) * SGU_CHUNK, :].astype(BF16)
        mixed = sb_ref[...]
        for g in range(SGU_GROUPS):
            mg = jnp.dot(w_tril[g], vn_c, preferred_element_type=F32)
            mixed = mixed + jnp.where(lane_w // HEAD_DIM == g, mg, 0.0)
        yc_ref[0, c * SGU_CHUNK:(c + 1) * SGU_CHUNK, :] = (
            u[c * SGU_CHUNK:(c + 1) * SGU_CHUNK, :] * mixed).astype(BF16)

    ext = jnp.concatenate([halo_ref[...], dp], axis=0)
    s2 = ext + pltpu.roll(ext, 1, 0)
    s4 = s2 + pltpu.roll(s2, 2, 0)
    s8 = s4 + pltpu.roll(s4, 4, 0)
    s16 = s8 + pltpu.roll(s8, 8, 0)
    lane_g = lax.broadcasted_iota(jnp.int32, (rows, GROUP_W), 1) // (GROUP_W // len(POOL_WINDOWS))
    t_glob = lax.broadcasted_iota(jnp.int32, (rows, GROUP_W), 0) + i * rows
    win = jnp.where(lane_g == 0, POOL_WINDOWS[0], jnp.where(lane_g == 1, POOL_WINDOWS[1],
          jnp.where(lane_g == 2, POOL_WINDOWS[2], POOL_WINDOWS[3])))
    wsum = jnp.where(lane_g == 0, s2[POOL_HALO:], jnp.where(lane_g == 1, s4[POOL_HALO:],
           jnp.where(lane_g == 2, s8[POOL_HALO:], s16[POOL_HALO:])))
    count = jnp.minimum(t_glob + 1, win).astype(F32)
    pooled = wsum / count - dp
    yd = jnp.dot(pooled.astype(BF16), pw_ref[...], preferred_element_type=F32) * ps_ref[...]
    yd_ref[0] = yd.astype(BF16)
    halo_ref[...] = dp[rows - POOL_HALO:, :]


def _mixer_front(x, w, cos_t, sin_a, sin_b, bfg, lng, lnb, sgu_w, sgu_bias, avg, pool_w, pool_s):
    bsz, seq, _ = x.shape
    n_blk = seq // MOBA_BLOCK
    assert seq % FRONT_ROWS == 0 and n_blk <= HEAD_PAD - HEAD_DIM and n_blk % 8 == 0
    rows = FRONT_ROWS
    const = lambda shape: pl.BlockSpec(shape, lambda b, i: (0,) * len(shape))
    head_spec = pl.BlockSpec((1, N_HEADS, rows, HEAD_PAD), lambda b, i: (b, 0, i, 0))
    flat_spec = pl.BlockSpec((1, rows, GROUP_W), lambda b, i: (b, i, 0))
    head_shape = jax.ShapeDtypeStruct((bsz, N_HEADS, seq, HEAD_PAD), BF16)
    flat_shape = jax.ShapeDtypeStruct((bsz, seq, GROUP_W), BF16)
    return pl.pallas_call(
        _front_kernel,
        grid=(bsz, seq // rows),
        in_specs=[
            pl.BlockSpec((1, rows, D_MODEL), lambda b, i: (b, i, 0)),
            const((D_MODEL, N_COLS)),
            pl.BlockSpec((rows, HEAD_PAD), lambda b, i: (i, 0)),
            pl.BlockSpec((rows, HEAD_PAD), lambda b, i: (i, 0)),
            pl.BlockSpec((rows, HEAD_PAD), lambda b, i: (i, 0)),
            const((1, 128)), const((1, GROUP_W)), const((1, GROUP_W)),
            const((SGU_GROUPS, SGU_CHUNK, SGU_CHUNK)), const((SGU_CHUNK, GROUP_W)),
            const((GROUP_W, GROUP_W)), const((GROUP_W, GROUP_W)), const((1, GROUP_W)),
        ],
        out_specs=[head_spec] * 6 + [flat_spec] * 2,
        out_shape=[head_shape] * 6 + [flat_shape] * 2,
        scratch_shapes=[
            pltpu.VMEM((N_HEADS, n_blk, HEAD_PAD), F32),
            pltpu.VMEM((8, 128), F32),
            pltpu.VMEM((POOL_HALO, GROUP_W), F32),
        ],
        compiler_params=pltpu.CompilerParams(
            dimension_semantics=("arbitrary", "arbitrary"), vmem_limit_bytes=VMEM_LIMIT),
        name="mixer_front",
    )(x, w, cos_t, sin_a, sin_b, bfg, lng, lnb, sgu_w, sgu_bias, avg, pool_w, pool_s)


def _attn_kernel(q_ref, k_ref, v_ref, o_ref):
    tq, tk = ATTN_TQ, ATTN_TK
    qi = pl.program_id(2)
    lane = lax.broadcasted_iota(jnp.int32, (tq, HEAD_PAD), 1)
    row = lax.broadcasted_iota(jnp.int32, (tq, tk), 0)
    col = lax.broadcasted_iota(jnp.int32, (tq, tk), 1)
    outs = []
    for h in range(2):
        q = q_ref[0, h]

        def tile(j, m, acc, diagonal, h=h, q=q):
            off = pl.multiple_of(j * tk, tk)
            k = k_ref[0, h, pl.ds(off, tk), :]
            v = v_ref[0, h, pl.ds(off, tk), :]
            s = lax.dot_general(q, k, (((1,), (1,)), ((), ())), preferred_element_type=F32)
            if diagonal:
                s = jnp.where(col <= row, s, -jnp.inf)
            m_new = jnp.maximum(m, jnp.max(s, axis=1, keepdims=True))
            alpha = jnp.exp(m - m_new)
            p = jnp.exp(s - m_new)
            acc = alpha * acc + jnp.dot(p.astype(BF16), v, preferred_element_type=F32)
            return m_new, acc

        m0 = jnp.full((tq, 1), -jnp.inf, F32)
        acc0 = jnp.zeros((tq, HEAD_PAD), F32)
        m, acc = lax.fori_loop(0, qi, lambda j, c: tile(j, c[0], c[1], False), (m0, acc0))
        m, acc = tile(qi, m, acc, True)
        outs.append(acc / acc[:, HEAD_DIM:HEAD_DIM + 1])
    o_ref[0] = jnp.where(lane < HEAD_DIM, outs[0], pltpu.roll(outs[1], HEAD_DIM, 1)).astype(BF16)


def _causal_attn(q, k, v):
    bsz, n_heads, seq, _ = q.shape
    assert ATTN_TQ == ATTN_TK and seq % ATTN_TQ == 0 and n_heads % 2 == 0
    kv_spec = pl.BlockSpec((1, 2, seq, HEAD_PAD), lambda b, hp, i: (b, hp, 0, 0))
    return pl.pallas_call(
        _attn_kernel,
        grid=(bsz, n_heads // 2, seq // ATTN_TQ),
        in_specs=[pl.BlockSpec((1, 2, ATTN_TQ, HEAD_PAD), lambda b, hp, i: (b, hp, i, 0)), kv_spec, kv_spec],
        out_specs=pl.BlockSpec((1, ATTN_TQ, 2 * HEAD_DIM), lambda b, hp, i: (b, i, hp)),
        out_shape=jax.ShapeDtypeStruct((bsz, seq, n_heads * HEAD_DIM), BF16),
        compiler_params=pltpu.CompilerParams(
            dimension_semantics=("parallel", "parallel", "arbitrary"), vmem_limit_bytes=VMEM_LIMIT),
        name="causal_attn",
    )(q, k, v)


def _out_proj_kernel(ya_ref, yb_ref, yc_ref, yd_ref, x_ref, w_ref, g_ref, b_ref, o_ref):
    y = jnp.concatenate([ya_ref[...], yb_ref[...], yc_ref[...], yd_ref[...]], axis=1)
    r = DN_ALPHA * x_ref[...] + jnp.dot(y, w_ref[...], preferred_element_type=F32)
    o_ref[...] = _layer_norm_rows(r, g_ref[...], b_ref[...])


def _out_proj_ln(ya, yb, yc, yd, x, w_o, g, b):
    m = x.shape[0]
    rows = PROJ_ROWS
    assert m % rows == 0
    part = pl.BlockSpec((rows, GROUP_W), lambda i: (i, 0))
    vec = pl.BlockSpec((1, D_MODEL), lambda i: (0, 0))
    return pl.pallas_call(
        _out_proj_kernel,
        grid=(m // rows,),
        in_specs=[part, part, part, part,
                  pl.BlockSpec((rows, D_MODEL), lambda i: (i, 0)),
                  pl.BlockSpec((D_MODEL, D_MODEL), lambda i: (0, 0)), vec, vec],
        out_specs=pl.BlockSpec((rows, D_MODEL), lambda i: (i, 0)),
        out_shape=jax.ShapeDtypeStruct((m, D_MODEL), F32),
        compiler_params=pltpu.CompilerParams(dimension_semantics=("parallel",), vmem_limit_bytes=VMEM_LIMIT),
        name="out_proj_ln",
    )(ya, yb, yc, yd, x, w_o, g, b)


def _ffn_kernel(tiles_per_seq, x_ref, xp_ref, wg_ref, wv_ref, cwg_ref, cwv_ref, cbg_ref, cbv_ref, wd_ref,
                g_ref, b_ref, o_ref, xb_ref, acc_ref):
    m = pl.program_id(0)
    n = pl.program_id(1)
    rows = FFN_ROWS

    @pl.when(n == 0)
    def _():
        prev = jnp.where(m % tiles_per_seq == 0, 0.0, xp_ref[...])
        xb_ref[0:FFN_HALO, :] = prev.astype(BF16)
        xb_ref[FFN_HALO:, :] = x_ref[...].astype(BF16)
        acc_ref[...] = jnp.zeros_like(acc_ref)

    xb = xb_ref[...]

    def conv(h, cw_ref, cb_ref):
        out = (pltpu.roll(h, 2, 0) * cw_ref[0:1, :] + pltpu.roll(h, 1, 0) * cw_ref[1:2, :]
               + h * cw_ref[2:3, :] + cb_ref[...])
        return out[FFN_HALO:, :]

    gate = conv(jnp.dot(xb, wg_ref[...], preferred_element_type=F32), cwg_ref, cbg_ref)
    val = conv(jnp.dot(xb, wv_ref[...], preferred_element_type=F32), cwv_ref, cbv_ref)
    act = gate * jax.nn.sigmoid(gate) * val
    acc_ref[...] += jnp.dot(act.astype(BF16), wd_ref[...], preferred_element_type=F32)

    @pl.when(n == pl.num_programs(1) - 1)
    def _():
        r = DN_ALPHA * x_ref[...] + acc_ref[...]
        o_ref[...] = _layer_norm_rows(r, g_ref[...], b_ref[...])


def _conv_ffn_ln(x, seq, w_up, conv_w, conv_b, w_down, g, b):
    m = x.shape[0]
    rows, cols = FFN_ROWS, FFN_COLS
    assert m % rows == 0 and seq % rows == 0 and D_FF % cols == 0 and rows % FFN_HALO == 0
    n_col = D_FF // cols
    halo_blocks = rows // FFN_HALO
    vec = pl.BlockSpec((1, D_MODEL), lambda i, j: (0, 0))
    return pl.pallas_call(
        functools.partial(_ffn_kernel, seq // rows),
        grid=(m // rows, n_col),
        in_specs=[
            pl.BlockSpec((rows, D_MODEL), lambda i, j: (i, 0)),
            pl.BlockSpec((FFN_HALO, D_MODEL), lambda i, j: (jnp.maximum(i * halo_blocks - 1, 0), 0)),
            pl.BlockSpec((D_MODEL, cols), lambda i, j: (0, j)),
            pl.BlockSpec((D_MODEL, cols), lambda i, j: (0, j + n_col)),
            pl.BlockSpec((3, cols), lambda i, j: (0, j)),
            pl.BlockSpec((3, cols), lambda i, j: (0, j + n_col)),
            pl.BlockSpec((1, cols), lambda i, j: (0, j)),
            pl.BlockSpec((1, cols), lambda i, j: (0, j + n_col)),
            pl.BlockSpec((cols, D_MODEL), lambda i, j: (j, 0)),
            vec, vec,
        ],
        out_specs=pl.BlockSpec((rows, D_MODEL), lambda i, j: (i, 0)),
        out_shape=jax.ShapeDtypeStruct((m, D_MODEL), F32),
        scratch_shapes=[pltpu.VMEM((FFN_HALO + rows, D_MODEL), BF16), pltpu.VMEM((rows, D_MODEL), F32)],
        compiler_params=pltpu.CompilerParams(
            dimension_semantics=("parallel", "arbitrary"), vmem_limit_bytes=VMEM_LIMIT),
        name="conv_ffn_ln",
    )(x, x, w_up, w_up, conv_w, conv_w, conv_b, conv_b, w_down, g, b)


def _pad_heads(w):
    d = w.shape[0]
    w = w.reshape(d, N_HEADS, HEAD_DIM)
    return jnp.pad(w, ((0, 0), (0, 0), (0, HEAD_PAD - HEAD_DIM))).reshape(d, N_HEADS * HEAD_PAD)


def _arrange_w_in(w_in):
    sizes = (GROUP_W,) * 6 + (N_HEADS,) + (GROUP_W,) * 3
    splits = [int(s) for s in np.cumsum(sizes)[:-1]]
    aq, ak, av, bq, bk, bv, bf, cu, cv, dp = jnp.split(w_in, splits, axis=-1)
    bf = jnp.pad(bf, ((0, 0), (0, 128 - N_HEADS)))
    cols = [_pad_heads(t) for t in (aq, ak, av, bq, bk, bv)] + [cu, cv, dp, bf]
    return jnp.concatenate(cols, axis=-1).astype(BF16)


def _rotary_lane_tables(seq):
    pos = jnp.arange(seq, dtype=F32)
    inv_freq = ROPE_THETA ** (-jnp.arange(0, ROPE_DIM, 2, dtype=F32) / ROPE_DIM)
    ang = pos[:, None] * inv_freq[None, :]
    cos, sin = jnp.cos(ang), jnp.sin(ang)
    half = ROPE_DIM // 2
    ones = jnp.ones((seq, HEAD_PAD - ROPE_DIM), F32)
    zeros = jnp.zeros((seq, HEAD_PAD - half), F32)
    cos_t = jnp.concatenate([cos, cos, ones], axis=1)
    sin_a = jnp.concatenate([-sin, zeros], axis=1)
    sin_b = jnp.concatenate([jnp.zeros((seq, half), F32), sin, jnp.zeros((seq, HEAD_PAD - ROPE_DIM), F32)], axis=1)
    return cos_t, sin_a, sin_b


def kernel(x, w_in, b_forget, sgu_ln_g, sgu_ln_b, sgu_w, sgu_b, pool_w, pool_scale, w_o, ln1_g, ln1_b,
           w_up, conv_w, conv_b, w_down, ln2_g, ln2_b):
    bsz, seq, _ = x.shape
    cos_t, sin_a, sin_b = _rotary_lane_tables(seq)
    group = jnp.arange(GROUP_W) // HEAD_DIM
    avg = jnp.where(group[:, None] == group[None, :], 1.0 / HEAD_DIM, 0.0).astype(BF16)
    xf = x.reshape(bsz * seq, D_MODEL)
    for l in range(DEPTH):
        bfg = jnp.pad(b_forget[l], (0, 128 - N_HEADS)).reshape(1, 128)
        sgu_bias = jnp.repeat(sgu_b[l].T, HEAD_DIM, axis=1)
        pool_bd = jax.scipy.linalg.block_diag(*[pool_w[l, g] for g in range(len(POOL_WINDOWS))]).astype(BF16)
        qa, ka, va, qb, kb, vb, yc, yd = _mixer_front(
            xf.reshape(bsz, seq, D_MODEL), _arrange_w_in(w_in[l]), cos_t, sin_a, sin_b, bfg,
            sgu_ln_g[l].reshape(1, GROUP_W), sgu_ln_b[l].reshape(1, GROUP_W), sgu_w[l], sgu_bias, avg,
            pool_bd, pool_scale[l].reshape(1, GROUP_W))
        ya = _causal_attn(qa, ka, va).reshape(bsz * seq, GROUP_W)
        yb = _causal_attn(qb, kb, vb).reshape(bsz * seq, GROUP_W)
        xf = _out_proj_ln(ya, yb, yc.reshape(bsz * seq, GROUP_W), yd.reshape(bsz * seq, GROUP_W), xf,
                          w_o[l].astype(BF16), ln1_g[l].reshape(1, D_MODEL), ln1_b[l].reshape(1, D_MODEL))
        xf = _conv_ffn_ln(xf, seq, w_up[l].astype(BF16), conv_w[l], conv_b[l].reshape(1, 2 * D_FF),
                          w_down[l].astype(BF16), ln2_g[l].reshape(1, D_MODEL), ln2_b[l].reshape(1, D_MODEL))
    return xf.reshape(bsz, seq, D_MODEL)
```

```python
import functools

import numpy as np
import jax
import jax.numpy as jnp
from jax import lax
from jax.experimental import pallas as pl
from jax.experimental.pallas import tpu as pltpu

F32 = jnp.float32
BF16 = jnp.bfloat16

D_MODEL = 1024
DEPTH = 2
HEAD_DIM = 64
N_HEADS = 4
GROUP_W = 256
HEAD_PAD = 128
MOBA_BLOCK = 256
MOBA_TOPK = 3
ROPE_THETA = 500000.0
ROPE_DIM = HEAD_DIM // 4
SGU_CHUNK = 128
SGU_GROUPS = 4
POOL_WINDOWS = (2, 4, 8, 16)
POOL_HALO = 16
D_FF = 2816
DN_ALPHA = (2 * DEPTH) ** 0.25
LN_EPS = 1e-5
ATTN_SCALE = HEAD_DIM ** -0.5
MASK_BIAS = -1e30

FRONT_ROWS = MOBA_BLOCK
ATTN_TQ = 512
ATTN_TK = 512
ATTN_HEADS = 2
PROJ_ROWS = 512
FFN_ROWS = 512
FFN_COLS = 256
FFN_HALO = 16
VMEM_LIMIT = 56 * 1024 * 1024

_QKV_W = N_HEADS * HEAD_PAD
COL_A = 0
COL_B = 3 * _QKV_W
COL_REST = 6 * _QKV_W
N_COLS = COL_REST + 3 * GROUP_W + 128


def _layer_norm_rows(r, g, b):
    mu = jnp.mean(r, axis=-1, keepdims=True)
    d = r - mu
    var = jnp.mean(d * d, axis=-1, keepdims=True)
    return d * lax.rsqrt(var + LN_EPS) * g + b


def _split_bf16(t):
    hi = t.astype(BF16)
    lo = (t - hi.astype(F32)).astype(BF16)
    return hi, lo


def _group_mean(t, avg):
    hi, lo = _split_bf16(t)
    return (jnp.dot(hi, avg, preferred_element_type=F32) + jnp.dot(lo, avg, preferred_element_type=F32))


def _front_kernel(x_ref, w_ref, cos_ref, sina_ref, sinb_ref, bfg_ref, lng_ref, lnb_ref, sw_ref, sb_ref,
                  avg_ref, pw_ref, ps_ref,
                  qa_ref, ka_ref, va_ref, qb_ref, kb_ref, vb_ref, yc_ref, yd_ref,
                  kbar_ref, carry_ref, halo_ref):
    rows = FRONT_ROWS
    i = pl.program_id(1)
    n_blk = kbar_ref.shape[1]

    @pl.when(i == 0)
    def _():
        kbar_ref[...] = jnp.zeros_like(kbar_ref)
        carry_ref[...] = jnp.zeros_like(carry_ref)
        halo_ref[...] = jnp.zeros_like(halo_ref)

    xb = x_ref[0].astype(BF16)
    lane = lax.broadcasted_iota(jnp.int32, (rows, HEAD_PAD), 1)

    za = jnp.dot(xb, w_ref[:, COL_A:COL_A + 3 * _QKV_W], preferred_element_type=F32)
    cos_t, sin_a, sin_b = cos_ref[...], sina_ref[...], sinb_ref[...]

    def rotary(t):
        return (t * cos_t + pltpu.roll(t, HEAD_PAD - ROPE_DIM // 2, 1) * sin_a
                + pltpu.roll(t, ROPE_DIM // 2, 1) * sin_b)

    blk_row = lax.broadcasted_iota(jnp.int32, (n_blk, rows), 0)
    for h in range(N_HEADS):
        q = rotary(za[:, h * HEAD_PAD:(h + 1) * HEAD_PAD])
        k = rotary(za[:, _QKV_W + h * HEAD_PAD:_QKV_W + (h + 1) * HEAD_PAD])
        v = za[:, 2 * _QKV_W + h * HEAD_PAD:2 * _QKV_W + (h + 1) * HEAD_PAD]
        gate_t = lax.dot_general(kbar_ref[h], q, (((1,), (1,)), ((), ())),
                                 precision=lax.Precision.HIGHEST, preferred_element_type=F32)
        g = jnp.where(blk_row < i, gate_t, -jnp.inf)
        chosen = jnp.zeros((n_blk, rows), F32)
        for _ in range(MOBA_TOPK):
            top = jnp.max(g, axis=0, keepdims=True)
            first = jnp.min(jnp.where(g == top, blk_row, n_blk), axis=0, keepdims=True)
            pick = blk_row == first
            chosen = jnp.where(pick, 1.0, chosen)
            g = jnp.where(pick, -jnp.inf, g)
        keep = jnp.where(blk_row < i, chosen, jnp.where(blk_row == i, 1.0, 0.0))
        bias_t = jnp.where(keep > 0.0, 0.0, MASK_BIAS)
        qa_ref[0, h] = jnp.concatenate(
            [(q * ATTN_SCALE).T[0:HEAD_DIM], bias_t, jnp.zeros((HEAD_PAD - HEAD_DIM - n_blk, rows), F32)],
            axis=0).astype(BF16)
        ka_ref[0, h] = jnp.where(lane == HEAD_DIM + i, 1.0, k).astype(BF16)
        va_ref[0, h, 0] = jnp.where(lane == HEAD_DIM, 1.0, v).T.astype(BF16)
        kbar_ref[h, pl.ds(i, 1), :] = jnp.sum(k, axis=0, keepdims=True) * (1.0 / MOBA_BLOCK)

    zr = jnp.dot(xb, w_ref[:, COL_REST:N_COLS], preferred_element_type=F32)
    cu = zr[:, 0:GROUP_W]
    cv = zr[:, GROUP_W:2 * GROUP_W]
    dp = zr[:, 2 * GROUP_W:3 * GROUP_W]
    fl = zr[:, 3 * GROUP_W:3 * GROUP_W + 128] + bfg_ref[...]
    log_f = jnp.minimum(fl, 0.0) - jnp.log1p(jnp.exp(-jnp.abs(fl)))
    log_f = jnp.where(lane < N_HEADS, log_f, 0.0)
    r_i = lax.broadcasted_iota(jnp.int32, (rows, rows), 0)
    c_i = lax.broadcasted_iota(jnp.int32, (rows, rows), 1)
    tri = jnp.where(c_i <= r_i, 1.0, 0.0)
    cum = jnp.dot(tri, log_f, precision=lax.Precision.HIGHEST, preferred_element_type=F32) + carry_ref[0:1, :]
    carry_ref[0:1, :] = cum[rows - 1:rows, :]

    zb = jnp.dot(xb, w_ref[:, COL_B:COL_B + 3 * _QKV_W], preferred_element_type=F32)
    cum_t = cum.T
    sub8 = lax.broadcasted_iota(jnp.int32, (8, rows), 0)
    for h in range(N_HEADS):
        q = zb[:, h * HEAD_PAD:(h + 1) * HEAD_PAD]
        k = zb[:, _QKV_W + h * HEAD_PAD:_QKV_W + (h + 1) * HEAD_PAD]
        v = zb[:, 2 * _QKV_W + h * HEAD_PAD:2 * _QKV_W + (h + 1) * HEAD_PAD]
        c = jnp.broadcast_to(cum[:, h:h + 1], (rows, HEAD_PAD))
        hi = c.astype(BF16).astype(F32)
        mid = (c - hi).astype(BF16).astype(F32)
        lo = c - hi - mid
        k_aug = jnp.where(lane < HEAD_DIM, k,
                jnp.where(lane < HEAD_DIM + 3, 1.0,
                jnp.where(lane == HEAD_DIM + 3, -hi,
                jnp.where(lane == HEAD_DIM + 4, -mid,
                jnp.where(lane == HEAD_DIM + 5, -lo, 0.0)))))
        c_t = jnp.broadcast_to(cum_t[h:h + 1, :], (8, rows))
        hi_t = c_t.astype(BF16).astype(F32)
        mid_t = (c_t - hi_t).astype(BF16).astype(F32)
        lo_t = c_t - hi_t - mid_t
        q_bias = jnp.where(sub8 == 0, hi_t, jnp.where(sub8 == 1, mid_t, jnp.where(sub8 == 2, lo_t,
                 jnp.where(sub8 < 6, 1.0, 0.0))))
        qb_ref[0, h] = jnp.concatenate(
            [(q * ATTN_SCALE).T[0:HEAD_DIM], q_bias, jnp.zeros((HEAD_PAD - HEAD_DIM - 8, rows), F32)],
            axis=0).astype(BF16)
        kb_ref[0, h] = k_aug.astype(BF16)
        vb_ref[0, h, 0] = jnp.where(lane == HEAD_DIM, 1.0, v).T.astype(BF16)

    inv_sqrt2 = np.float32(1.0 / np.sqrt(2.0))
    u = 0.5 * cu * (1.0 + lax.erf(cu * inv_sqrt2))
    vg = 0.5 * cv * (1.0 + lax.erf(cv * inv_sqrt2))
    avg = avg_ref[...]
    mu = _group_mean(vg, avg)
    dv = vg - mu
    var = _group_mean(dv * dv, avg)
    vn = dv * lax.rsqrt(var + LN_EPS) * lng_ref[...] + lnb_ref[...]
    lane_w = lax.broadcasted_iota(jnp.int32, (SGU_CHUNK, GROUP_W), 1)
    t_i = lax.broadcasted_iota(jnp.int32, (SGU_CHUNK, SGU_CHUNK), 0)
    s_i = lax.broadcasted_iota(jnp.int32, (SGU_CHUNK, SGU_CHUNK), 1)
    w_tril = [jnp.where(s_i <= t_i, sw_ref[g], 0.0).astype(BF16) for g in range(SGU_GROUPS)]
    for c in range(rows // SGU_CHUNK):
        vn_c = vn[c * SGU_CHUNK:(c + 1) * SGU_CHUNK, :].astype(BF16)
        mixed = sb_ref[...]
        for g in range(SGU_GROUPS):
            mg = jnp.dot(w_tril[g], vn_c, preferred_element_type=F32)
            mixed = mixed + jnp.where(lane_w // HEAD_DIM == g, mg, 0.0)
        yc_ref[0, c * SGU_CHUNK:(c + 1) * SGU_CHUNK, :] = (
            u[c * SGU_CHUNK:(c + 1) * SGU_CHUNK, :] * mixed).astype(BF16)

    ext = jnp.concatenate([halo_ref[...], dp], axis=0)
    s2 = ext + pltpu.roll(ext, 1, 0)
    s4 = s2 + pltpu.roll(s2, 2, 0)
    s8 = s4 + pltpu.roll(s4, 4, 0)
    s16 = s8 + pltpu.roll(s8, 8, 0)
    lane_g = lax.broadcasted_iota(jnp.int32, (rows, GROUP_W), 1) // (GROUP_W // len(POOL_WINDOWS))
    t_glob = lax.broadcasted_iota(jnp.int32, (rows, GROUP_W), 0) + i * rows
    win = jnp.where(lane_g == 0, POOL_WINDOWS[0], jnp.where(lane_g == 1, POOL_WINDOWS[1],
          jnp.where(lane_g == 2, POOL_WINDOWS[2], POOL_WINDOWS[3])))
    wsum = jnp.where(lane_g == 0, s2[POOL_HALO:], jnp.where(lane_g == 1, s4[POOL_HALO:],
           jnp.where(lane_g == 2, s8[POOL_HALO:], s16[POOL_HALO:])))
    count = jnp.minimum(t_glob + 1, win).astype(F32)
    pooled = wsum / count - dp
    yd = jnp.dot(pooled.astype(BF16), pw_ref[...], preferred_element_type=F32) * ps_ref[...]
    yd_ref[0] = yd.astype(BF16)
    halo_ref[...] = dp[rows - POOL_HALO:, :]


def _mixer_front(x, w, cos_t, sin_a, sin_b, bfg, lng, lnb, sgu_w, sgu_bias, avg, pool_w, pool_s):
    bsz, seq, _ = x.shape
    n_blk = seq // MOBA_BLOCK
    assert seq % ATTN_TK == 0 and ATTN_TK % FRONT_ROWS == 0
    assert n_blk <= HEAD_PAD - HEAD_DIM and n_blk % 8 == 0
    rows = FRONT_ROWS
    const = lambda shape: pl.BlockSpec(shape, lambda b, i: (0,) * len(shape))
    per_tk = ATTN_TK // rows
    q_spec = pl.BlockSpec((1, N_HEADS, HEAD_PAD, rows), lambda b, i: (b, 0, 0, i))
    k_spec = pl.BlockSpec((1, N_HEADS, rows, HEAD_PAD), lambda b, i: (b, 0, i, 0))
    v_spec = pl.BlockSpec((1, N_HEADS, 1, HEAD_PAD, rows), lambda b, i: (b, 0, i // per_tk, 0, i % per_tk))
    flat_spec = pl.BlockSpec((1, rows, GROUP_W), lambda b, i: (b, i, 0))
    q_shape = jax.ShapeDtypeStruct((bsz, N_HEADS, HEAD_PAD, seq), BF16)
    k_shape = jax.ShapeDtypeStruct((bsz, N_HEADS, seq, HEAD_PAD), BF16)
    v_shape = jax.ShapeDtypeStruct((bsz, N_HEADS, seq // ATTN_TK, HEAD_PAD, ATTN_TK), BF16)
    flat_shape = jax.ShapeDtypeStruct((bsz, seq, GROUP_W), BF16)
    return pl.pallas_call(
        _front_kernel,
        grid=(bsz, seq // rows),
        in_specs=[
            pl.BlockSpec((1, rows, D_MODEL), lambda b, i: (b, i, 0)),
            const((D_MODEL, N_COLS)),
            pl.BlockSpec((rows, HEAD_PAD), lambda b, i: (i, 0)),
            pl.BlockSpec((rows, HEAD_PAD), lambda b, i: (i, 0)),
            pl.BlockSpec((rows, HEAD_PAD), lambda b, i: (i, 0)),
            const((1, 128)), const((1, GROUP_W)), const((1, GROUP_W)),
            const((SGU_GROUPS, SGU_CHUNK, SGU_CHUNK)), const((SGU_CHUNK, GROUP_W)),
            const((GROUP_W, GROUP_W)), const((GROUP_W, GROUP_W)), const((1, GROUP_W)),
        ],
        out_specs=[q_spec, k_spec, v_spec] * 2 + [flat_spec] * 2,
        out_shape=[q_shape, k_shape, v_shape] * 2 + [flat_shape] * 2,
        scratch_shapes=[
            pltpu.VMEM((N_HEADS, n_blk, HEAD_PAD), F32),
            pltpu.VMEM((8, 128), F32),
            pltpu.VMEM((POOL_HALO, GROUP_W), F32),
        ],
        compiler_params=pltpu.CompilerParams(
            dimension_semantics=("arbitrary", "arbitrary"), vmem_limit_bytes=VMEM_LIMIT),
        name="mixer_front",
    )(x, w, cos_t, sin_a, sin_b, bfg, lng, lnb, sgu_w, sgu_bias, avg, pool_w, pool_s)


def _attn_kernel(qt_ref, k_ref, vt_ref, o_ref, acc_ref, sa_ref, sb_ref, mta_ref, mtb_ref):
    tq, tk = ATTN_TQ, ATTN_TK
    qi = pl.program_id(2)
    n_tiles = k_ref.shape[2] // tk
    acc_ref[...] = jnp.zeros_like(acc_ref)

    def produce(j, s_ref, mt_ref):
        off = pl.multiple_of(j * tk, tk)
        for h in range(ATTN_HEADS):
            s = jnp.dot(k_ref[0, h, pl.ds(off, tk), :], qt_ref[0, h], preferred_element_type=F32)
            s_ref[h] = s
            mt_ref[h, 0:1, :] = jnp.max(s, axis=0, keepdims=True)

    def consume(j, s_ref, mt_ref, ms, masked):
        new_ms = []
        for h in range(ATTN_HEADS):
            if masked:
                key_pos = lax.broadcasted_iota(jnp.int32, (tk, tq), 0) + j * tk
                qry_pos = lax.broadcasted_iota(jnp.int32, (tk, tq), 1) + qi * tq
                s = jnp.where(key_pos <= qry_pos, s_ref[h], -jnp.inf)
                m_tile = jnp.max(s, axis=0, keepdims=True)
            else:
                s = s_ref[h]
                m_tile = mt_ref[h, 0:1, :]
            m_new = jnp.maximum(ms[h], m_tile)
            alpha = jnp.exp(ms[h] - m_new)
            p = jnp.exp(s - m_new).astype(BF16)
            j_v = jnp.minimum(j, n_tiles - 1)
            acc_ref[h] = alpha * acc_ref[h] + jnp.dot(vt_ref[0, h, j_v], p, preferred_element_type=F32)
            new_ms.append(m_new)
        return tuple(new_ms)

    def pair(jj, ms):
        produce(2 * jj + 1, sb_ref, mtb_ref)
        ms = consume(2 * jj, sa_ref, mta_ref, ms, False)
        produce(2 * jj + 2, sa_ref, mta_ref)
        return consume(2 * jj + 1, sb_ref, mtb_ref, ms, False)

    n_pairs = qi // 2
    produce(0, sa_ref, mta_ref)
    m0 = jnp.full((1, tq), -jnp.inf, F32)
    ms = lax.fori_loop(0, n_pairs, pair, (m0,) * ATTN_HEADS)
    produce(jnp.minimum(2 * n_pairs + 1, n_tiles - 1), sb_ref, mtb_ref)
    ms = consume(2 * n_pairs, sa_ref, mta_ref, ms, True)

    @pl.when(qi % 2 == 1)
    def _():
        consume(2 * n_pairs + 1, sb_ref, mtb_ref, ms, True)

    outs = [(acc_ref[h] / acc_ref[h, HEAD_DIM:HEAD_DIM + 1, :])[0:HEAD_DIM] for h in range(ATTN_HEADS)]
    o_ref[0] = jnp.concatenate(outs, axis=0).T.astype(BF16)


def _causal_attn(qt, k, vt):
    bsz, n_heads, seq, _ = k.shape
    assert ATTN_TQ == ATTN_TK and seq % ATTN_TQ == 0 and n_heads % ATTN_HEADS == 0
    return pl.pallas_call(
        _attn_kernel,
        grid=(bsz, n_heads // ATTN_HEADS, seq // ATTN_TQ),
        in_specs=[
            pl.BlockSpec((1, ATTN_HEADS, HEAD_PAD, ATTN_TQ), lambda b, hp, i: (b, hp, 0, i)),
            pl.BlockSpec((1, ATTN_HEADS, seq, HEAD_PAD), lambda b, hp, i: (b, hp, 0, 0)),
            pl.BlockSpec((1, ATTN_HEADS, seq // ATTN_TK, HEAD_PAD, ATTN_TK), lambda b, hp, i: (b, hp, 0, 0, 0)),
        ],
        out_specs=pl.BlockSpec((1, ATTN_TQ, ATTN_HEADS * HEAD_DIM), lambda b, hp, i: (b, i, hp)),
        out_shape=jax.ShapeDtypeStruct((bsz, seq, n_heads * HEAD_DIM), BF16),
        scratch_shapes=[pltpu.VMEM((ATTN_HEADS, HEAD_PAD, ATTN_TQ), F32),
                        pltpu.VMEM((ATTN_HEADS, ATTN_TK, ATTN_TQ), F32),
                        pltpu.VMEM((ATTN_HEADS, ATTN_TK, ATTN_TQ), F32),
                        pltpu.VMEM((ATTN_HEADS, 8, ATTN_TQ), F32),
                        pltpu.VMEM((ATTN_HEADS, 8, ATTN_TQ), F32)],
        compiler_params=pltpu.CompilerParams(
            dimension_semantics=("parallel", "parallel", "arbitrary"), vmem_limit_bytes=VMEM_LIMIT),
        name="causal_attn",
    )(qt, k, vt)


def _out_proj_kernel(ya_ref, yb_ref, yc_ref, yd_ref, x_ref, w_ref, g_ref, b_ref, o_ref):
    y = jnp.concatenate([ya_ref[...], yb_ref[...], yc_ref[...], yd_ref[...]], axis=1)
    r = DN_ALPHA * x_ref[...] + jnp.dot(y, w_ref[...], preferred_element_type=F32)
    o_ref[...] = _layer_norm_rows(r, g_ref[...], b_ref[...])


def _out_proj_ln(ya, yb, yc, yd, x, w_o, g, b):
    m = x.shape[0]
    rows = PROJ_ROWS
    assert m % rows == 0
    part = pl.BlockSpec((rows, GROUP_W), lambda i: (i, 0))
    vec = pl.BlockSpec((1, D_MODEL), lambda i: (0, 0))
    return pl.pallas_call(
        _out_proj_kernel,
        grid=(m // rows,),
        in_specs=[part, part, part, part,
                  pl.BlockSpec((rows, D_MODEL), lambda i: (i, 0)),
                  pl.BlockSpec((D_MODEL, D_MODEL), lambda i: (0, 0)), vec, vec],
        out_specs=pl.BlockSpec((rows, D_MODEL), lambda i: (i, 0)),
        out_shape=jax.ShapeDtypeStruct((m, D_MODEL), F32),
        compiler_params=pltpu.CompilerParams(dimension_semantics=("parallel",), vmem_limit_bytes=VMEM_LIMIT),
        name="out_proj_ln",
    )(ya, yb, yc, yd, x, w_o, g, b)


def _ffn_kernel(tiles_per_seq, x_ref, xp_ref, wg_ref, wv_ref, cwg_ref, cwv_ref, cbg_ref, cbv_ref, wd_ref,
                g_ref, b_ref, o_ref, xb_ref, acc_ref):
    m = pl.program_id(0)
    n = pl.program_id(1)
    rows = FFN_ROWS

    @pl.when(n == 0)
    def _():
        prev = jnp.where(m % tiles_per_seq == 0, 0.0, xp_ref[...])
        xb_ref[0:FFN_HALO, :] = prev.astype(BF16)
        xb_ref[FFN_HALO:, :] = x_ref[...].astype(BF16)
        acc_ref[...] = jnp.zeros_like(acc_ref)

    xb = xb_ref[...]

    def conv(h, cw_ref, cb_ref):
        out = (pltpu.roll(h, 2, 0) * cw_ref[0:1, :] + pltpu.roll(h, 1, 0) * cw_ref[1:2, :]
               + h * cw_ref[2:3, :] + cb_ref[...])
        return out[FFN_HALO:, :]

    gate = conv(jnp.dot(xb, wg_ref[...], preferred_element_type=F32), cwg_ref, cbg_ref)
    val = conv(jnp.dot(xb, wv_ref[...], preferred_element_type=F32), cwv_ref, cbv_ref)
    act = gate * jax.nn.sigmoid(gate) * val
    acc_ref[...] += jnp.dot(act.astype(BF16), wd_ref[...], preferred_element_type=F32)

    @pl.when(n == pl.num_programs(1) - 1)
    def _():
        r = DN_ALPHA * x_ref[...] + acc_ref[...]
        o_ref[...] = _layer_norm_rows(r, g_ref[...], b_ref[...])


def _conv_ffn_ln(x, seq, w_up, conv_w, conv_b, w_down, g, b):
    m = x.shape[0]
    rows, cols = FFN_ROWS, FFN_COLS
    assert m % rows == 0 and seq % rows == 0 and D_FF % cols == 0 and rows % FFN_HALO == 0
    n_col = D_FF // cols
    halo_blocks = rows // FFN_HALO
    vec = pl.BlockSpec((1, D_MODEL), lambda i, j: (0, 0))
    return pl.pallas_call(
        functools.partial(_ffn_kernel, seq // rows),
        grid=(m // rows, n_col),
        in_specs=[
            pl.BlockSpec((rows, D_MODEL), lambda i, j: (i, 0)),
            pl.BlockSpec((FFN_HALO, D_MODEL), lambda i, j: (jnp.maximum(i * halo_blocks - 1, 0), 0)),
            pl.BlockSpec((D_MODEL, cols), lambda i, j: (0, j)),
            pl.BlockSpec((D_MODEL, cols), lambda i, j: (0, j + n_col)),
            pl.BlockSpec((3, cols), lambda i, j: (0, j)),
            pl.BlockSpec((3, cols), lambda i, j: (0, j + n_col)),
            pl.BlockSpec((1, cols), lambda i, j: (0, j)),
            pl.BlockSpec((1, cols), lambda i, j: (0, j + n_col)),
            pl.BlockSpec((cols, D_MODEL), lambda i, j: (j, 0)),
            vec, vec,
        ],
        out_specs=pl.BlockSpec((rows, D_MODEL), lambda i, j: (i, 0)),
        out_shape=jax.ShapeDtypeStruct((m, D_MODEL), F32),
        scratch_shapes=[pltpu.VMEM((FFN_HALO + rows, D_MODEL), BF16), pltpu.VMEM((rows, D_MODEL), F32)],
        compiler_params=pltpu.CompilerParams(
            dimension_semantics=("parallel", "arbitrary"), vmem_limit_bytes=VMEM_LIMIT),
        name="conv_ffn_ln",
    )(x, x, w_up, w_up, conv_w, conv_w, conv_b, conv_b, w_down, g, b)


def _pad_heads(w):
    d = w.shape[0]
    w = w.reshape(d, N_HEADS, HEAD_DIM)
    return jnp.pad(w, ((0, 0), (0, 0), (0, HEAD_PAD - HEAD_DIM))).reshape(d, N_HEADS * HEAD_PAD)


def _arrange_w_in(w_in):
    sizes = (GROUP_W,) * 6 + (N_HEADS,) + (GROUP_W,) * 3
    splits = [int(s) for s in np.cumsum(sizes)[:-1]]
    aq, ak, av, bq, bk, bv, bf, cu, cv, dp = jnp.split(w_in, splits, axis=-1)
    bf = jnp.pad(bf, ((0, 0), (0, 128 - N_HEADS)))
    cols = [_pad_heads(t) for t in (aq, ak, av, bq, bk, bv)] + [cu, cv, dp, bf]
    return jnp.concatenate(cols, axis=-1).astype(BF16)


def _rotary_lane_tables(seq):
    pos = jnp.arange(seq, dtype=F32)
    inv_freq = ROPE_THETA ** (-jnp.arange(0, ROPE_DIM, 2, dtype=F32) / ROPE_DIM)
    ang = pos[:, None] * inv_freq[None, :]
    cos, sin = jnp.cos(ang), jnp.sin(ang)
    half = ROPE_DIM // 2
    ones = jnp.ones((seq, HEAD_PAD - ROPE_DIM), F32)
    zeros = jnp.zeros((seq, HEAD_PAD - half), F32)
    cos_t = jnp.concatenate([cos, cos, ones], axis=1)
    sin_a = jnp.concatenate([-sin, zeros], axis=1)
    sin_b = jnp.concatenate([jnp.zeros((seq, half), F32), sin, jnp.zeros((seq, HEAD_PAD - ROPE_DIM), F32)], axis=1)
    return cos_t, sin_a, sin_b


def kernel(x, w_in, b_forget, sgu_ln_g, sgu_ln_b, sgu_w, sgu_b, pool_w, pool_scale, w_o, ln1_g, ln1_b,
           w_up, conv_w, conv_b, w_down, ln2_g, ln2_b):
    bsz, seq, _ = x.shape
    cos_t, sin_a, sin_b = _rotary_lane_tables(seq)
    group = jnp.arange(GROUP_W) // HEAD_DIM
    avg = jnp.where(group[:, None] == group[None, :], 1.0 / HEAD_DIM, 0.0).astype(BF16)
    xf = x.reshape(bsz * seq, D_MODEL)
    for l in range(DEPTH):
        bfg = jnp.pad(b_forget[l], (0, 128 - N_HEADS)).reshape(1, 128)
        sgu_bias = jnp.repeat(sgu_b[l].T, HEAD_DIM, axis=1)
        pool_bd = jax.scipy.linalg.block_diag(*[pool_w[l, g] for g in range(len(POOL_WINDOWS))]).astype(BF16)
        qa, ka, va, qb, kb, vb, yc, yd = _mixer_front(
            xf.reshape(bsz, seq, D_MODEL), _arrange_w_in(w_in[l]), cos_t, sin_a, sin_b, bfg,
            sgu_ln_g[l].reshape(1, GROUP_W), sgu_ln_b[l].reshape(1, GROUP_W), sgu_w[l], sgu_bias, avg,
            pool_bd, pool_scale[l].reshape(1, GROUP_W))
        ya = _causal_attn(qa, ka, va).reshape(bsz * seq, GROUP_W)
        yb = _causal_attn(qb, kb, vb).reshape(bsz * seq, GROUP_W)
        xf = _out_proj_ln(ya, yb, yc.reshape(bsz * seq, GROUP_W), yd.reshape(bsz * seq, GROUP_W), xf,
                          w_o[l].astype(BF16), ln1_g[l].reshape(1, D_MODEL), ln1_b[l].reshape(1, D_MODEL))
        xf = _conv_ffn_ln(xf, seq, w_up[l].astype(BF16), conv_w[l], conv_b[l].reshape(1, 2 * D_FF),
                          w_down[l].astype(BF16), ln2_g[l].reshape(1, D_MODEL), ln2_b[l].reshape(1, D_MODEL))
    return xf.reshape(bsz, seq, D_MODEL)
```

```python
import functools

import numpy as np
import jax
import jax.numpy as jnp
from jax import lax
from jax.experimental import pallas as pl
from jax.experimental.pallas import tpu as pltpu

F32 = jnp.float32
BF16 = jnp.bfloat16

D_MODEL = 1024
DEPTH = 2
HEAD_DIM = 64
N_HEADS = 4
GROUP_W = 256
HEAD_PAD = 128
PAIR_W = 2 * HEAD_DIM
V_ROWS = 80
MOBA_BLOCK = 256
MOBA_TOPK = 3
ROPE_THETA = 500000.0
ROPE_DIM = HEAD_DIM // 4
SGU_CHUNK = 128
SGU_GROUPS = 4
POOL_WINDOWS = (2, 4, 8, 16)
POOL_HALO = 16
D_FF = 2816
DN_ALPHA = (2 * DEPTH) ** 0.25
LN_EPS = 1e-5
ATTN_SCALE = HEAD_DIM ** -0.5
MASK_BIAS = -1e30

FRONT_ROWS = MOBA_BLOCK
ATTN_TQ = 512
ATTN_TK = 256
ATTN_HEADS = 2
PROJ_ROWS = 512
FFN_ROWS = 512
FFN_COLS = 256
FFN_HALO = 16
VMEM_LIMIT = 56 * 1024 * 1024

COL_A = 0
COL_B = 3 * GROUP_W
COL_REST = 6 * GROUP_W
N_COLS = COL_REST + 3 * GROUP_W + 128


def _layer_norm_rows(r, g, b):
    mu = jnp.mean(r, axis=-1, keepdims=True)
    d = r - mu
    var = jnp.mean(d * d, axis=-1, keepdims=True)
    return d * lax.rsqrt(var + LN_EPS) * g + b


def _split_bf16(t):
    hi = t.astype(BF16)
    lo = (t - hi.astype(F32)).astype(BF16)
    return hi, lo


def _group_mean(t, avg):
    hi, lo = _split_bf16(t)
    return (jnp.dot(hi, avg, preferred_element_type=F32) + jnp.dot(lo, avg, preferred_element_type=F32))


def _front_kernel(x_ref, w_ref, cos_ref, sina_ref, sinb_ref, bfg_ref, lng_ref, lnb_ref, sw_ref, sb_ref,
                  avg_ref, pw_ref, ps_ref,
                  qa_ref, ka_ref, va_ref, qb_ref, kb_ref, vb_ref, yc_ref, yd_ref,
                  kbar_ref, carry_ref, halo_ref):
    rows = FRONT_ROWS
    i = pl.program_id(1)
    n_blk = kbar_ref.shape[1]

    @pl.when(i == 0)
    def _():
        kbar_ref[...] = jnp.zeros_like(kbar_ref)
        carry_ref[...] = jnp.zeros_like(carry_ref)
        halo_ref[...] = jnp.zeros_like(halo_ref)

    xb = x_ref[0].astype(BF16)
    lane = lax.broadcasted_iota(jnp.int32, (rows, PAIR_W), 1)
    ones_slab = jnp.where(lax.broadcasted_iota(jnp.int32, (V_ROWS - HEAD_DIM, rows), 0) == 0, 1.0, 0.0)

    def head_rows(pair_t, e):
        return pair_t[e * HEAD_DIM:(e + 1) * HEAD_DIM]

    def head_lanes_first(pair, e):
        return pair if e == 0 else pltpu.roll(pair, HEAD_DIM, 1)

    za = jnp.dot(xb, w_ref[:, COL_A:COL_A + 3 * GROUP_W], preferred_element_type=F32)
    cos_t, sin_a, sin_b = cos_ref[...], sina_ref[...], sinb_ref[...]

    def rotary(t):
        return (t * cos_t + pltpu.roll(t, PAIR_W - ROPE_DIM // 2, 1) * sin_a
                + pltpu.roll(t, ROPE_DIM // 2, 1) * sin_b)

    blk_row = lax.broadcasted_iota(jnp.int32, (n_blk, rows), 0)
    blk_lane = lax.broadcasted_iota(jnp.int32, (n_blk, PAIR_W), 1)
    for pr in range(N_HEADS // 2):
        q2 = rotary(za[:, pr * PAIR_W:(pr + 1) * PAIR_W])
        k2 = rotary(za[:, GROUP_W + pr * PAIR_W:GROUP_W + (pr + 1) * PAIR_W])
        q2_t = (q2 * ATTN_SCALE).T
        v2_t = za[:, 2 * GROUP_W + pr * PAIR_W:2 * GROUP_W + (pr + 1) * PAIR_W].T
        for e in range(2):
            h = 2 * pr + e
            kbar = jnp.where(blk_lane // HEAD_DIM == e, kbar_ref[pr], 0.0)
            gate_t = lax.dot_general(kbar, q2, (((1,), (1,)), ((), ())),
                                     precision=lax.Precision.HIGHEST, preferred_element_type=F32)
            g = jnp.where(blk_row < i, gate_t, -jnp.inf)
            chosen = jnp.zeros((n_blk, rows), F32)
            for _ in range(MOBA_TOPK):
                top = jnp.max(g, axis=0, keepdims=True)
                first = jnp.min(jnp.where(g == top, blk_row, n_blk), axis=0, keepdims=True)
                pick = blk_row == first
                chosen = jnp.where(pick, 1.0, chosen)
                g = jnp.where(pick, -jnp.inf, g)
            keep = jnp.where(blk_row < i, chosen, jnp.where(blk_row == i, 1.0, 0.0))
            bias_t = jnp.where(keep > 0.0, 0.0, MASK_BIAS)
            qa_ref[0, h] = jnp.concatenate(
                [head_rows(q2_t, e), bias_t, jnp.zeros((HEAD_PAD - HEAD_DIM - n_blk, rows), F32)],
                axis=0).astype(BF16)
            ka_ref[0, h] = jnp.where(lane < HEAD_DIM, head_lanes_first(k2, e),
                                     jnp.where(lane == HEAD_DIM + i, 1.0, 0.0)).astype(BF16)
            va_ref[0, h, 0] = jnp.concatenate([head_rows(v2_t, e), ones_slab], axis=0).astype(BF16)
        kbar_ref[pr, pl.ds(i, 1), :] = jnp.sum(k2, axis=0, keepdims=True) * (1.0 / MOBA_BLOCK)

    zr = jnp.dot(xb, w_ref[:, COL_REST:N_COLS], preferred_element_type=F32)
    cu = zr[:, 0:GROUP_W]
    cv = zr[:, GROUP_W:2 * GROUP_W]
    dp = zr[:, 2 * GROUP_W:3 * GROUP_W]
    fl = zr[:, 3 * GROUP_W:3 * GROUP_W + 128] + bfg_ref[...]
    log_f = jnp.minimum(fl, 0.0) - jnp.log1p(jnp.exp(-jnp.abs(fl)))
    log_f = jnp.where(lane < N_HEADS, log_f, 0.0)
    r_i = lax.broadcasted_iota(jnp.int32, (rows, rows), 0)
    c_i = lax.broadcasted_iota(jnp.int32, (rows, rows), 1)
    tri = jnp.where(c_i <= r_i, 1.0, 0.0)
    cum = jnp.dot(tri, log_f, precision=lax.Precision.HIGHEST, preferred_element_type=F32) + carry_ref[0:1, :]
    carry_ref[0:1, :] = cum[rows - 1:rows, :]

    zb = jnp.dot(xb, w_ref[:, COL_B:COL_B + 3 * GROUP_W], preferred_element_type=F32)
    cum_t = cum.T
    sub8 = lax.broadcasted_iota(jnp.int32, (8, rows), 0)
    for pr in range(N_HEADS // 2):
        q2_t = (zb[:, pr * PAIR_W:(pr + 1) * PAIR_W] * ATTN_SCALE).T
        k2 = zb[:, GROUP_W + pr * PAIR_W:GROUP_W + (pr + 1) * PAIR_W]
        v2_t = zb[:, 2 * GROUP_W + pr * PAIR_W:2 * GROUP_W + (pr + 1) * PAIR_W].T
        for e in range(2):
            h = 2 * pr + e
            c = jnp.broadcast_to(cum[:, h:h + 1], (rows, PAIR_W))
            hi = c.astype(BF16).astype(F32)
            mid = (c - hi).astype(BF16).astype(F32)
            lo = c - hi - mid
            k_aug = jnp.where(lane < HEAD_DIM, head_lanes_first(k2, e),
                    jnp.where(lane < HEAD_DIM + 3, 1.0,
                    jnp.where(lane == HEAD_DIM + 3, -hi,
                    jnp.where(lane == HEAD_DIM + 4, -mid,
                    jnp.where(lane == HEAD_DIM + 5, -lo, 0.0)))))
            c_t = jnp.broadcast_to(cum_t[h:h + 1, :], (8, rows))
            hi_t = c_t.astype(BF16).astype(F32)
            mid_t = (c_t - hi_t).astype(BF16).astype(F32)
            lo_t = c_t - hi_t - mid_t
            q_bias = jnp.where(sub8 == 0, hi_t, jnp.where(sub8 == 1, mid_t, jnp.where(sub8 == 2, lo_t,
                     jnp.where(sub8 < 6, 1.0, 0.0))))
            qb_ref[0, h] = jnp.concatenate(
                [head_rows(q2_t, e), q_bias, jnp.zeros((HEAD_PAD - HEAD_DIM - 8, rows), F32)],
                axis=0).astype(BF16)
            kb_ref[0, h] = k_aug.astype(BF16)
            vb_ref[0, h, 0] = jnp.concatenate([head_rows(v2_t, e), ones_slab], axis=0).astype(BF16)

    inv_sqrt2 = np.float32(1.0 / np.sqrt(2.0))
    u = 0.5 * cu * (1.0 + lax.erf(cu * inv_sqrt2))
    vg = 0.5 * cv * (1.0 + lax.erf(cv * inv_sqrt2))
    avg = avg_ref[...]
    mu = _group_mean(vg, avg)
    dv = vg - mu
    var = _group_mean(dv * dv, avg)
    vn = dv * lax.rsqrt(var + LN_EPS) * lng_ref[...] + lnb_ref[...]
    lane_w = lax.broadcasted_iota(jnp.int32, (SGU_CHUNK, GROUP_W), 1)
    t_i = lax.broadcasted_iota(jnp.int32, (SGU_CHUNK, SGU_CHUNK), 0)
    s_i = lax.broadcasted_iota(jnp.int32, (SGU_CHUNK, SGU_CHUNK), 1)
    w_tril = [jnp.where(s_i <= t_i, sw_ref[g], 0.0).astype(BF16) for g in range(SGU_GROUPS)]
    for c in range(rows // SGU_CHUNK):
        vn_c = vn[c * SGU_CHUNK:(c + 1) * SGU_CHUNK, :].astype(BF16)
        mixed = sb_ref[...]
        for g in range(SGU_GROUPS):
            mg = jnp.dot(w_tril[g], vn_c, preferred_element_type=F32)
            mixed = mixed + jnp.where(lane_w // HEAD_DIM == g, mg, 0.0)
        yc_ref[0, c * SGU_CHUNK:(c + 1) * SGU_CHUNK, :] = (
            u[c * SGU_CHUNK:(c + 1) * SGU_CHUNK, :] * mixed).astype(BF16)

    ext = jnp.concatenate([halo_ref[...], dp], axis=0)
    s2 = ext + pltpu.roll(ext, 1, 0)
    s4 = s2 + pltpu.roll(s2, 2, 0)
    s8 = s4 + pltpu.roll(s4, 4, 0)
    s16 = s8 + pltpu.roll(s8, 8, 0)
    lane_g = lax.broadcasted_iota(jnp.int32, (rows, GROUP_W), 1) // (GROUP_W // len(POOL_WINDOWS))
    t_glob = lax.broadcasted_iota(jnp.int32, (rows, GROUP_W), 0) + i * rows
    win = jnp.where(lane_g == 0, POOL_WINDOWS[0], jnp.where(lane_g == 1, POOL_WINDOWS[1],
          jnp.where(lane_g == 2, POOL_WINDOWS[2], POOL_WINDOWS[3])))
    wsum = jnp.where(lane_g == 0, s2[POOL_HALO:], jnp.where(lane_g == 1, s4[POOL_HALO:],
           jnp.where(lane_g == 2, s8[POOL_HALO:], s16[POOL_HALO:])))
    count = jnp.minimum(t_glob + 1, win).astype(F32)
    pooled = wsum / count - dp
    yd = jnp.dot(pooled.astype(BF16), pw_ref[...], preferred_element_type=F32) * ps_ref[...]
    yd_ref[0] = yd.astype(BF16)
    halo_ref[...] = dp[rows - POOL_HALO:, :]


def _mixer_front(x, w, cos_t, sin_a, sin_b, bfg, lng, lnb, sgu_w, sgu_bias, avg, pool_w, pool_s):
    bsz, seq, _ = x.shape
    n_blk = seq // MOBA_BLOCK
    assert seq % ATTN_TK == 0 and ATTN_TK % FRONT_ROWS == 0
    assert n_blk <= HEAD_PAD - HEAD_DIM and n_blk % 8 == 0
    rows = FRONT_ROWS
    const = lambda shape: pl.BlockSpec(shape, lambda b, i: (0,) * len(shape))
    per_tk = ATTN_TK // rows
    q_spec = pl.BlockSpec((1, N_HEADS, HEAD_PAD, rows), lambda b, i: (b, 0, 0, i))
    k_spec = pl.BlockSpec((1, N_HEADS, rows, HEAD_PAD), lambda b, i: (b, 0, i, 0))
    v_spec = pl.BlockSpec((1, N_HEADS, 1, V_ROWS, rows), lambda b, i: (b, 0, i // per_tk, 0, i % per_tk))
    flat_spec = pl.BlockSpec((1, rows, GROUP_W), lambda b, i: (b, i, 0))
    q_shape = jax.ShapeDtypeStruct((bsz, N_HEADS, HEAD_PAD, seq), BF16)
    k_shape = jax.ShapeDtypeStruct((bsz, N_HEADS, seq, HEAD_PAD), BF16)
    v_shape = jax.ShapeDtypeStruct((bsz, N_HEADS, seq // ATTN_TK, V_ROWS, ATTN_TK), BF16)
    flat_shape = jax.ShapeDtypeStruct((bsz, seq, GROUP_W), BF16)
    return pl.pallas_call(
        _front_kernel,
        grid=(bsz, seq // rows),
        in_specs=[
            pl.BlockSpec((1, rows, D_MODEL), lambda b, i: (b, i, 0)),
            const((D_MODEL, N_COLS)),
            pl.BlockSpec((rows, PAIR_W), lambda b, i: (i, 0)),
            pl.BlockSpec((rows, PAIR_W), lambda b, i: (i, 0)),
            pl.BlockSpec((rows, PAIR_W), lambda b, i: (i, 0)),
            const((1, 128)), const((1, GROUP_W)), const((1, GROUP_W)),
            const((SGU_GROUPS, SGU_CHUNK, SGU_CHUNK)), const((SGU_CHUNK, GROUP_W)),
            const((GROUP_W, GROUP_W)), const((GROUP_W, GROUP_W)), const((1, GROUP_W)),
        ],
        out_specs=[q_spec, k_spec, v_spec] * 2 + [flat_spec] * 2,
        out_shape=[q_shape, k_shape, v_shape] * 2 + [flat_shape] * 2,
        scratch_shapes=[
            pltpu.VMEM((N_HEADS // 2, n_blk, PAIR_W), F32),
            pltpu.VMEM((8, 128), F32),
            pltpu.VMEM((POOL_HALO, GROUP_W), F32),
        ],
        compiler_params=pltpu.CompilerParams(
            dimension_semantics=("arbitrary", "arbitrary"), vmem_limit_bytes=VMEM_LIMIT),
        name="mixer_front",
    )(x, w, cos_t, sin_a, sin_b, bfg, lng, lnb, sgu_w, sgu_bias, avg, pool_w, pool_s)


def _attn_kernel(qt_ref, k_ref, vt_ref, o_ref, acc_ref, sa_ref, sb_ref, mta_ref, mtb_ref):
    tq, tk = ATTN_TQ, ATTN_TK
    qi = pl.program_id(2)
    acc_ref[...] = jnp.zeros_like(acc_ref)

    def produce(j, s_ref, mt_ref):
        off = pl.multiple_of(j * tk, tk)
        for h in range(ATTN_HEADS):
            s = jnp.dot(k_ref[0, h, pl.ds(off, tk), :], qt_ref[0, h], preferred_element_type=F32)
            s_ref[h] = s
            mt_ref[h, 0:1, :] = jnp.max(s, axis=0, keepdims=True)

    def consume(j, s_ref, mt_ref, ms, masked):
        new_ms = []
        for h in range(ATTN_HEADS):
            if masked:
                key_pos = lax.broadcasted_iota(jnp.int32, (tk, tq), 0) + j * tk
                qry_pos = lax.broadcasted_iota(jnp.int32, (tk, tq), 1) + qi * tq
                s = jnp.where(key_pos <= qry_pos, s_ref[h], -jnp.inf)
                m_tile = jnp.max(s, axis=0, keepdims=True)
            else:
                s = s_ref[h]
                m_tile = mt_ref[h, 0:1, :]
            m_new = jnp.maximum(ms[h], m_tile)
            alpha = jnp.exp(ms[h] - m_new)
            p = jnp.exp(s - m_new).astype(BF16)
            acc_ref[h] = alpha * acc_ref[h] + jnp.dot(vt_ref[0, h, j], p, preferred_element_type=F32)
            new_ms.append(m_new)
        return tuple(new_ms)

    def pair(jj, ms):
        produce(2 * jj + 1, sb_ref, mtb_ref)
        ms = consume(2 * jj, sa_ref, mta_ref, ms, False)
        produce(2 * jj + 2, sa_ref, mta_ref)
        return consume(2 * jj + 1, sb_ref, mtb_ref, ms, False)

    produce(0, sa_ref, mta_ref)
    m0 = jnp.full((1, tq), -jnp.inf, F32)
    ms = lax.fori_loop(0, qi, pair, (m0,) * ATTN_HEADS)
    produce(2 * qi + 1, sb_ref, mtb_ref)
    ms = consume(2 * qi, sa_ref, mta_ref, ms, True)
    consume(2 * qi + 1, sb_ref, mtb_ref, ms, True)

    outs = [(acc_ref[h] / acc_ref[h, HEAD_DIM:HEAD_DIM + 1, :])[0:HEAD_DIM] for h in range(ATTN_HEADS)]
    o_ref[0] = jnp.concatenate(outs, axis=0).T.astype(BF16)


def _causal_attn(qt, k, vt):
    bsz, n_heads, seq, _ = k.shape
    assert ATTN_TQ == 2 * ATTN_TK and seq % ATTN_TQ == 0 and n_heads % ATTN_HEADS == 0
    return pl.pallas_call(
        _attn_kernel,
        grid=(bsz, n_heads // ATTN_HEADS, seq // ATTN_TQ),
        in_specs=[
            pl.BlockSpec((1, ATTN_HEADS, HEAD_PAD, ATTN_TQ), lambda b, hp, i: (b, hp, 0, i)),
            pl.BlockSpec((1, ATTN_HEADS, seq, HEAD_PAD), lambda b, hp, i: (b, hp, 0, 0)),
            pl.BlockSpec((1, ATTN_HEADS, seq // ATTN_TK, V_ROWS, ATTN_TK), lambda b, hp, i: (b, hp, 0, 0, 0)),
        ],
        out_specs=pl.BlockSpec((1, ATTN_TQ, ATTN_HEADS * HEAD_DIM), lambda b, hp, i: (b, i, hp)),
        out_shape=jax.ShapeDtypeStruct((bsz, seq, n_heads * HEAD_DIM), BF16),
        scratch_shapes=[pltpu.VMEM((ATTN_HEADS, V_ROWS, ATTN_TQ), F32),
                        pltpu.VMEM((ATTN_HEADS, ATTN_TK, ATTN_TQ), F32),
                        pltpu.VMEM((ATTN_HEADS, ATTN_TK, ATTN_TQ), F32),
                        pltpu.VMEM((ATTN_HEADS, 8, ATTN_TQ), F32),
                        pltpu.VMEM((ATTN_HEADS, 8, ATTN_TQ), F32)],
        compiler_params=pltpu.CompilerParams(
            dimension_semantics=("parallel", "parallel", "arbitrary"), vmem_limit_bytes=VMEM_LIMIT),
        name="causal_attn",
    )(qt, k, vt)


def _out_proj_kernel(ya_ref, yb_ref, yc_ref, yd_ref, x_ref, w_ref, g_ref, b_ref, o_ref):
    y = jnp.concatenate([ya_ref[...], yb_ref[...], yc_ref[...], yd_ref[...]], axis=1)
    r = DN_ALPHA * x_ref[...] + jnp.dot(y, w_ref[...], preferred_element_type=F32)
    o_ref[...] = _layer_norm_rows(r, g_ref[...], b_ref[...])


def _out_proj_ln(ya, yb, yc, yd, x, w_o, g, b):
    m = x.shape[0]
    rows = PROJ_ROWS
    assert m % rows == 0
    part = pl.BlockSpec((rows, GROUP_W), lambda i: (i, 0))
    vec = pl.BlockSpec((1, D_MODEL), lambda i: (0, 0))
    return pl.pallas_call(
        _out_proj_kernel,
        grid=(m // rows,),
        in_specs=[part, part, part, part,
                  pl.BlockSpec((rows, D_MODEL), lambda i: (i, 0)),
                  pl.BlockSpec((D_MODEL, D_MODEL), lambda i: (0, 0)), vec, vec],
        out_specs=pl.BlockSpec((rows, D_MODEL), lambda i: (i, 0)),
        out_shape=jax.ShapeDtypeStruct((m, D_MODEL), F32),
        compiler_params=pltpu.CompilerParams(dimension_semantics=("parallel",), vmem_limit_bytes=VMEM_LIMIT),
        name="out_proj_ln",
    )(ya, yb, yc, yd, x, w_o, g, b)


def _ffn_kernel(tiles_per_seq, x_ref, xp_ref, wu_ref, cw_ref, cb_ref, wd_ref, g_ref, b_ref, o_ref,
                xb_ref, ha_ref, hb_ref, acc_ref):
    m = pl.program_id(0)
    n_col = wu_ref.shape[0]
    prev = jnp.where(m % tiles_per_seq == 0, 0.0, xp_ref[...])
    xb_ref[0:FFN_HALO, :] = prev.astype(BF16)
    xb_ref[FFN_HALO:, :] = x_ref[...].astype(BF16)

    def up(n, h_ref):
        h_ref[...] = jnp.dot(xb_ref[...], wu_ref[n], preferred_element_type=F32)

    def gated(n, h_ref):
        h = h_ref[...]
        cw = cw_ref[n]
        c = (pltpu.roll(h, 2, 0) * cw[0:1, :] + pltpu.roll(h, 1, 0) * cw[1:2, :] + h * cw[2:3, :]
             + cb_ref[n])[FFN_HALO:, :]
        gate, val = c[:, 0:FFN_COLS], c[:, FFN_COLS:2 * FFN_COLS]
        return (gate * jax.nn.sigmoid(gate) * val).astype(BF16)

    def down(n, act):
        return jnp.dot(act, wd_ref[n], preferred_element_type=F32)

    def pair(p, carry):
        up(2 * p + 1, hb_ref)
        act0 = gated(2 * p, ha_ref)
        up(2 * p + 2, ha_ref)
        act1 = gated(2 * p + 1, hb_ref)
        acc_ref[...] += down(2 * p, act0) + down(2 * p + 1, act1)
        return carry

    up(0, ha_ref)
    acc_ref[...] = jnp.zeros_like(acc_ref)
    lax.fori_loop(0, (n_col - 1) // 2, pair, 0)
    y = acc_ref[...] + down(n_col - 1, gated(n_col - 1, ha_ref))
    o_ref[...] = _layer_norm_rows(DN_ALPHA * x_ref[...] + y, g_ref[...], b_ref[...])


def _conv_ffn_ln(x, seq, w_up, conv_w, conv_b, w_down, g, b):
    m = x.shape[0]
    rows, cols = FFN_ROWS, FFN_COLS
    n_col = w_up.shape[0]
    assert m % rows == 0 and seq % rows == 0 and rows % FFN_HALO == 0 and n_col % 2 == 1
    halo_blocks = rows // FFN_HALO
    resident = lambda shape: pl.BlockSpec(shape, lambda i: (0,) * len(shape), pipeline_mode=pl.Buffered(1))
    return pl.pallas_call(
        functools.partial(_ffn_kernel, seq // rows),
        grid=(m // rows,),
        in_specs=[
            pl.BlockSpec((rows, D_MODEL), lambda i: (i, 0)),
            pl.BlockSpec((FFN_HALO, D_MODEL), lambda i: (jnp.maximum(i * halo_blocks - 1, 0), 0)),
            resident((n_col, D_MODEL, 2 * cols)),
            resident((n_col, 3, 2 * cols)),
            resident((n_col, 1, 2 * cols)),
            resident((n_col, cols, D_MODEL)),
            resident((1, D_MODEL)), resident((1, D_MODEL)),
        ],
        out_specs=pl.BlockSpec((rows, D_MODEL), lambda i: (i, 0)),
        out_shape=jax.ShapeDtypeStruct((m, D_MODEL), F32),
        scratch_shapes=[pltpu.VMEM((FFN_HALO + rows, D_MODEL), BF16),
                        pltpu.VMEM((FFN_HALO + rows, 2 * cols), F32),
                        pltpu.VMEM((FFN_HALO + rows, 2 * cols), F32),
                        pltpu.VMEM((rows, D_MODEL), F32)],
        compiler_params=pltpu.CompilerParams(dimension_semantics=("parallel",), vmem_limit_bytes=VMEM_LIMIT),
        name="conv_ffn_ln",
    )(x, x, w_up, conv_w, conv_b, w_down, g, b)


def _arrange_w_in(w_in):
    f0 = 6 * GROUP_W
    forget = jnp.pad(w_in[:, f0:f0 + N_HEADS], ((0, 0), (0, 128 - N_HEADS)))
    return jnp.concatenate([w_in[:, :f0], w_in[:, f0 + N_HEADS:], forget], axis=-1).astype(BF16)


def _tile_gate_value(t):
    lead = t.shape[:-1]
    t = t.reshape(lead + (2, D_FF // FFN_COLS, FFN_COLS))
    t = jnp.moveaxis(t, -2, 0)
    return t.reshape((D_FF // FFN_COLS,) + lead + (2 * FFN_COLS,))


def _rotary_lane_tables(seq):
    pos = jnp.arange(seq, dtype=F32)
    inv_freq = ROPE_THETA ** (-jnp.arange(0, ROPE_DIM, 2, dtype=F32) / ROPE_DIM)
    ang = pos[:, None] * inv_freq[None, :]
    cos, sin = jnp.cos(ang), jnp.sin(ang)
    half = ROPE_DIM // 2
    ones = jnp.ones((seq, HEAD_DIM - ROPE_DIM), F32)
    zeros = jnp.zeros((seq, HEAD_DIM - half), F32)
    cos_t = jnp.concatenate([cos, cos, ones], axis=1)
    sin_a = jnp.concatenate([-sin, zeros], axis=1)
    sin_b = jnp.concatenate([jnp.zeros((seq, half), F32), sin, jnp.zeros((seq, HEAD_DIM - ROPE_DIM), F32)], axis=1)
    return tuple(jnp.concatenate([t, t], axis=1) for t in (cos_t, sin_a, sin_b))


def kernel(x, w_in, b_forget, sgu_ln_g, sgu_ln_b, sgu_w, sgu_b, pool_w, pool_scale, w_o, ln1_g, ln1_b,
           w_up, conv_w, conv_b, w_down, ln2_g, ln2_b):
    bsz, seq, _ = x.shape
    cos_t, sin_a, sin_b = _rotary_lane_tables(seq)
    group = jnp.arange(GROUP_W) // HEAD_DIM
    avg = jnp.where(group[:, None] == group[None, :], 1.0 / HEAD_DIM, 0.0).astype(BF16)
    xf = x.reshape(bsz * seq, D_MODEL)
    for l in range(DEPTH):
        bfg = jnp.pad(b_forget[l], (0, 128 - N_HEADS)).reshape(1, 128)
        sgu_bias = jnp.repeat(sgu_b[l].T, HEAD_DIM, axis=1)
        pool_bd = jax.scipy.linalg.block_diag(*[pool_w[l, g] for g in range(len(POOL_WINDOWS))]).astype(BF16)
        qa, ka, va, qb, kb, vb, yc, yd = _mixer_front(
            xf.reshape(bsz, seq, D_MODEL), _arrange_w_in(w_in[l]), cos_t, sin_a, sin_b, bfg,
            sgu_ln_g[l].reshape(1, GROUP_W), sgu_ln_b[l].reshape(1, GROUP_W), sgu_w[l], sgu_bias, avg,
            pool_bd, pool_scale[l].reshape(1, GROUP_W))
        ya = _causal_attn(qa, ka, va).reshape(bsz * seq, GROUP_W)
        yb = _causal_attn(qb, kb, vb).reshape(bsz * seq, GROUP_W)
        xf = _out_proj_ln(ya, yb, yc.reshape(bsz * seq, GROUP_W), yd.reshape(bsz * seq, GROUP_W), xf,
                          w_o[l].astype(BF16), ln1_g[l].reshape(1, D_MODEL), ln1_b[l].reshape(1, D_MODEL))
        xf = _conv_ffn_ln(xf, seq, _tile_gate_value(w_up[l]).astype(BF16), _tile_gate_value(conv_w[l]),
                          _tile_gate_value(conv_b[l].reshape(1, 2 * D_FF)),
                          w_down[l].astype(BF16).reshape(D_FF // FFN_COLS, FFN_COLS, D_MODEL),
                          ln2_g[l].reshape(1, D_MODEL), ln2_b[l].reshape(1, D_MODEL))
    return xf.reshape(bsz, seq, D_MODEL)
```

```python
import functools

import numpy as np
import jax
import jax.numpy as jnp
from jax import lax
from jax.experimental import pallas as pl
from jax.experimental.pallas import tpu as pltpu

F32 = jnp.float32
BF16 = jnp.bfloat16

D_MODEL = 1024
DEPTH = 2
HEAD_DIM = 64
N_HEADS = 4
GROUP_W = 256
HEAD_PAD = 128
PAIR_W = 2 * HEAD_DIM
V_ROWS = 80
MOBA_BLOCK = 256
MOBA_TOPK = 3
ROPE_THETA = 500000.0
ROPE_DIM = HEAD_DIM // 4
SGU_CHUNK = 128
SGU_GROUPS = 4
POOL_WINDOWS = (2, 4, 8, 16)
POOL_HALO = 16
D_FF = 2816
DN_ALPHA = (2 * DEPTH) ** 0.25
LN_EPS = 1e-5
ATTN_SCALE = HEAD_DIM ** -0.5
MASK_BIAS = -1e30

FRONT_ROWS = MOBA_BLOCK
ATTN_TQ = 512
ATTN_TK = 256
ATTN_HEADS = 2
PROJ_ROWS = 512
FFN_ROWS = 512
FFN_COLS = 256
FFN_HALO = 16
VMEM_LIMIT = 56 * 1024 * 1024

COL_A = 0
COL_B = 3 * GROUP_W
COL_FORGET = 6 * GROUP_W
REST_W = 3 * GROUP_W + 128


def _layer_norm_rows(r, g, b):
    mu = jnp.mean(r, axis=-1, keepdims=True)
    d = r - mu
    var = jnp.mean(d * d, axis=-1, keepdims=True)
    return d * lax.rsqrt(var + LN_EPS) * g + b


def _split_bf16(t):
    hi = t.astype(BF16)
    lo = (t - hi.astype(F32)).astype(BF16)
    return hi, lo


def _group_mean(t, avg):
    hi, lo = _split_bf16(t)
    return (jnp.dot(hi, avg, preferred_element_type=F32) + jnp.dot(lo, avg, preferred_element_type=F32))


def _front_kernel(x_ref, wqkv_ref, wrest_ref, rot_ref, bfg_ref, lng_ref, lnb_ref, sw_ref, sb_ref,
                  avg_ref, pw_ref, ps_ref,
                  qa_ref, ka_ref, va_ref, qb_ref, kb_ref, vb_ref, yc_ref, yd_ref,
                  kbar_ref, carry_ref, halo_ref):
    rows = FRONT_ROWS
    i = pl.program_id(1)
    n_blk = kbar_ref.shape[1]

    @pl.when(i == 0)
    def _():
        kbar_ref[...] = jnp.zeros_like(kbar_ref)
        carry_ref[...] = jnp.zeros_like(carry_ref)
        halo_ref[...] = jnp.zeros_like(halo_ref)

    xb = x_ref[0].astype(BF16)
    lane = lax.broadcasted_iota(jnp.int32, (rows, PAIR_W), 1)
    ones_slab = jnp.where(lax.broadcasted_iota(jnp.int32, (V_ROWS - HEAD_DIM, rows), 0) == 0, 1.0, 0.0)

    def head_rows(pair_t, e):
        return pair_t[e * HEAD_DIM:(e + 1) * HEAD_DIM]

    def head_lanes_first(pair, e):
        return pair if e == 0 else pltpu.roll(pair, HEAD_DIM, 1)

    za = jnp.dot(xb, wqkv_ref[:, COL_A:COL_A + 3 * GROUP_W], preferred_element_type=F32)
    cos_t, sin_a, sin_b = rot_ref[0], rot_ref[1], rot_ref[2]

    def rotary(t):
        return (t * cos_t + pltpu.roll(t, PAIR_W - ROPE_DIM // 2, 1) * sin_a
                + pltpu.roll(t, ROPE_DIM // 2, 1) * sin_b)

    blk_row = lax.broadcasted_iota(jnp.int32, (n_blk, rows), 0)
    blk_lane = lax.broadcasted_iota(jnp.int32, (n_blk, PAIR_W), 1)
    for pr in range(N_HEADS // 2):
        q2 = rotary(za[:, pr * PAIR_W:(pr + 1) * PAIR_W])
        k2 = rotary(za[:, GROUP_W + pr * PAIR_W:GROUP_W + (pr + 1) * PAIR_W])
        q2_t = (q2 * ATTN_SCALE).T
        v2_t = za[:, 2 * GROUP_W + pr * PAIR_W:2 * GROUP_W + (pr + 1) * PAIR_W].T
        for e in range(2):
            h = 2 * pr + e
            kbar = jnp.where(blk_lane // HEAD_DIM == e, kbar_ref[pr], 0.0)
            gate_t = lax.dot_general(kbar, q2, (((1,), (1,)), ((), ())),
                                     precision=lax.Precision.HIGHEST, preferred_element_type=F32)
            g = jnp.where(blk_row < i, gate_t, -jnp.inf)
            chosen = jnp.zeros((n_blk, rows), F32)
            for _ in range(MOBA_TOPK):
                top = jnp.max(g, axis=0, keepdims=True)
                first = jnp.min(jnp.where(g == top, blk_row, n_blk), axis=0, keepdims=True)
                pick = blk_row == first
                chosen = jnp.where(pick, 1.0, chosen)
                g = jnp.where(pick, -jnp.inf, g)
            keep = jnp.where(blk_row < i, chosen, jnp.where(blk_row == i, 1.0, 0.0))
            bias_t = jnp.where(keep > 0.0, 0.0, MASK_BIAS)
            qa_ref[0, h] = jnp.concatenate(
                [head_rows(q2_t, e), bias_t, jnp.zeros((HEAD_PAD - HEAD_DIM - n_blk, rows), F32)],
                axis=0).astype(BF16)
            ka_ref[0, h] = jnp.where(lane < HEAD_DIM, head_lanes_first(k2, e),
                                     jnp.where(lane == HEAD_DIM + i, 1.0, 0.0)).astype(BF16)
            va_ref[0, h, 0] = jnp.concatenate([head_rows(v2_t, e), ones_slab], axis=0).astype(BF16)
        kbar_ref[pr, pl.ds(i, 1), :] = jnp.sum(k2, axis=0, keepdims=True) * (1.0 / MOBA_BLOCK)

    zr = jnp.dot(xb, wrest_ref[...], preferred_element_type=F32)
    cu = zr[:, 0:GROUP_W]
    cv = zr[:, GROUP_W:2 * GROUP_W]
    dp = zr[:, 2 * GROUP_W:3 * GROUP_W]
    fl = zr[:, 3 * GROUP_W:3 * GROUP_W + 128] + bfg_ref[...]
    log_f = jnp.minimum(fl, 0.0) - jnp.log1p(jnp.exp(-jnp.abs(fl)))
    log_f = jnp.where(lane < N_HEADS, log_f, 0.0)
    r_i = lax.broadcasted_iota(jnp.int32, (rows, rows), 0)
    c_i = lax.broadcasted_iota(jnp.int32, (rows, rows), 1)
    tri = jnp.where(c_i <= r_i, 1.0, 0.0).astype(BF16)
    f_hi = log_f.astype(BF16)
    f_res = log_f - f_hi.astype(F32)
    f_mid = f_res.astype(BF16)
    f_lo = (f_res - f_mid.astype(F32)).astype(BF16)
    cum = (jnp.dot(tri, f_hi, preferred_element_type=F32) + jnp.dot(tri, f_mid, preferred_element_type=F32)
           + jnp.dot(tri, f_lo, preferred_element_type=F32) + carry_ref[0:1, :])
    carry_ref[0:1, :] = cum[rows - 1:rows, :]

    zb = jnp.dot(xb, wqkv_ref[:, COL_B:COL_B + 3 * GROUP_W], preferred_element_type=F32)
    cum_t = cum.T
    sub8 = lax.broadcasted_iota(jnp.int32, (8, rows), 0)
    for pr in range(N_HEADS // 2):
        q2_t = (zb[:, pr * PAIR_W:(pr + 1) * PAIR_W] * ATTN_SCALE).T
        k2 = zb[:, GROUP_W + pr * PAIR_W:GROUP_W + (pr + 1) * PAIR_W]
        v2_t = zb[:, 2 * GROUP_W + pr * PAIR_W:2 * GROUP_W + (pr + 1) * PAIR_W].T
        for e in range(2):
            h = 2 * pr + e
            c = jnp.broadcast_to(cum[:, h:h + 1], (rows, PAIR_W))
            hi = c.astype(BF16).astype(F32)
            mid = (c - hi).astype(BF16).astype(F32)
            lo = c - hi - mid
            k_aug = jnp.where(lane < HEAD_DIM, head_lanes_first(k2, e),
                    jnp.where(lane < HEAD_DIM + 3, 1.0,
                    jnp.where(lane == HEAD_DIM + 3, -hi,
                    jnp.where(lane == HEAD_DIM + 4, -mid,
                    jnp.where(lane == HEAD_DIM + 5, -lo, 0.0)))))
            c_t = jnp.broadcast_to(cum_t[h:h + 1, :], (8, rows))
            hi_t = c_t.astype(BF16).astype(F32)
            mid_t = (c_t - hi_t).astype(BF16).astype(F32)
            lo_t = c_t - hi_t - mid_t
            q_bias = jnp.where(sub8 == 0, hi_t, jnp.where(sub8 == 1, mid_t, jnp.where(sub8 == 2, lo_t,
                     jnp.where(sub8 < 6, 1.0, 0.0))))
            qb_ref[0, h] = jnp.concatenate(
                [head_rows(q2_t, e), q_bias, jnp.zeros((HEAD_PAD - HEAD_DIM - 8, rows), F32)],
                axis=0).astype(BF16)
            kb_ref[0, h] = k_aug.astype(BF16)
            vb_ref[0, h, 0] = jnp.concatenate([head_rows(v2_t, e), ones_slab], axis=0).astype(BF16)

    inv_sqrt2 = np.float32(1.0 / np.sqrt(2.0))
    u = 0.5 * cu * (1.0 + lax.erf(cu * inv_sqrt2))
    vg = 0.5 * cv * (1.0 + lax.erf(cv * inv_sqrt2))
    avg = avg_ref[...]
    mu = _group_mean(vg, avg)
    dv = vg - mu
    var = _group_mean(dv * dv, avg)
    vn = dv * lax.rsqrt(var + LN_EPS) * lng_ref[...] + lnb_ref[...]
    lane_w = lax.broadcasted_iota(jnp.int32, (SGU_CHUNK, GROUP_W), 1)
    t_i = lax.broadcasted_iota(jnp.int32, (SGU_CHUNK, SGU_CHUNK), 0)
    s_i = lax.broadcasted_iota(jnp.int32, (SGU_CHUNK, SGU_CHUNK), 1)
    w_tril = [jnp.where(s_i <= t_i, sw_ref[g], 0.0).astype(BF16) for g in range(SGU_GROUPS)]
    for c in range(rows // SGU_CHUNK):
        vn_c = vn[c * SGU_CHUNK:(c + 1) * SGU_CHUNK, :].astype(BF16)
        mixed = sb_ref[...]
        for g in range(SGU_GROUPS):
            mg = jnp.dot(w_tril[g], vn_c, preferred_element_type=F32)
            mixed = mixed + jnp.where(lane_w // HEAD_DIM == g, mg, 0.0)
        yc_ref[0, c * SGU_CHUNK:(c + 1) * SGU_CHUNK, :] = (
            u[c * SGU_CHUNK:(c + 1) * SGU_CHUNK, :] * mixed).astype(BF16)

    ext = jnp.concatenate([halo_ref[...], dp], axis=0)
    s2 = ext + pltpu.roll(ext, 1, 0)
    s4 = s2 + pltpu.roll(s2, 2, 0)
    s8 = s4 + pltpu.roll(s4, 4, 0)
    s16 = s8 + pltpu.roll(s8, 8, 0)
    lane_g = lax.broadcasted_iota(jnp.int32, (rows, GROUP_W), 1) // (GROUP_W // len(POOL_WINDOWS))
    t_glob = lax.broadcasted_iota(jnp.int32, (rows, GROUP_W), 0) + i * rows
    win = jnp.where(lane_g == 0, POOL_WINDOWS[0], jnp.where(lane_g == 1, POOL_WINDOWS[1],
          jnp.where(lane_g == 2, POOL_WINDOWS[2], POOL_WINDOWS[3])))
    wsum = jnp.where(lane_g == 0, s2[POOL_HALO:], jnp.where(lane_g == 1, s4[POOL_HALO:],
           jnp.where(lane_g == 2, s8[POOL_HALO:], s16[POOL_HALO:])))
    count = jnp.minimum(t_glob + 1, win).astype(F32)
    pooled = wsum / count - dp
    yd = jnp.dot(pooled.astype(BF16), pw_ref[...], preferred_element_type=F32) * ps_ref[...]
    yd_ref[0] = yd.astype(BF16)
    halo_ref[...] = dp[rows - POOL_HALO:, :]


def _mixer_front(x, layer, w_in, w_rest, rot, bfg, lng, lnb, sgu_w, sgu_bias, avg, pool_w, pool_s):
    bsz, seq, _ = x.shape
    n_blk = seq // MOBA_BLOCK
    assert seq % ATTN_TK == 0 and ATTN_TK % FRONT_ROWS == 0
    assert n_blk <= HEAD_PAD - HEAD_DIM and n_blk % 8 == 0
    rows = FRONT_ROWS
    const = lambda *shape: pl.BlockSpec(shape, lambda b, i: (0,) * len(shape))
    of_layer = lambda *shape: pl.BlockSpec((None,) + shape, lambda b, i: (layer,) + (0,) * len(shape))
    per_tk = ATTN_TK // rows
    q_spec = pl.BlockSpec((1, N_HEADS, HEAD_PAD, rows), lambda b, i: (b, 0, 0, i))
    k_spec = pl.BlockSpec((1, N_HEADS, rows, HEAD_PAD), lambda b, i: (b, 0, i, 0))
    v_spec = pl.BlockSpec((1, N_HEADS, 1, V_ROWS, rows), lambda b, i: (b, 0, i // per_tk, 0, i % per_tk))
    flat_spec = pl.BlockSpec((1, rows, GROUP_W), lambda b, i: (b, i, 0))
    q_shape = jax.ShapeDtypeStruct((bsz, N_HEADS, HEAD_PAD, seq), BF16)
    k_shape = jax.ShapeDtypeStruct((bsz, N_HEADS, seq, HEAD_PAD), BF16)
    v_shape = jax.ShapeDtypeStruct((bsz, N_HEADS, seq // ATTN_TK, V_ROWS, ATTN_TK), BF16)
    flat_shape = jax.ShapeDtypeStruct((bsz, seq, GROUP_W), BF16)
    return pl.pallas_call(
        _front_kernel,
        grid=(bsz, seq // rows),
        in_specs=[
            pl.BlockSpec((1, rows, D_MODEL), lambda b, i: (b, i, 0)),
            of_layer(D_MODEL, COL_FORGET),
            of_layer(D_MODEL, REST_W),
            pl.BlockSpec((3, rows, PAIR_W), lambda b, i: (0, i, 0)),
            of_layer(1, 128), of_layer(1, GROUP_W), of_layer(1, GROUP_W),
            of_layer(SGU_GROUPS, SGU_CHUNK, SGU_CHUNK), of_layer(SGU_CHUNK, GROUP_W),
            const(GROUP_W, GROUP_W), of_layer(GROUP_W, GROUP_W), of_layer(1, GROUP_W),
        ],
        out_specs=[q_spec, k_spec, v_spec] * 2 + [flat_spec] * 2,
        out_shape=[q_shape, k_shape, v_shape] * 2 + [flat_shape] * 2,
        scratch_shapes=[
            pltpu.VMEM((N_HEADS // 2, n_blk, PAIR_W), F32),
            pltpu.VMEM((8, 128), F32),
            pltpu.VMEM((POOL_HALO, GROUP_W), F32),
        ],
        compiler_params=pltpu.CompilerParams(
            dimension_semantics=("arbitrary", "arbitrary"), vmem_limit_bytes=VMEM_LIMIT),
        name="mixer_front",
    )(x, w_in, w_rest, rot, bfg, lng, lnb, sgu_w, sgu_bias, avg, pool_w, pool_s)


def _attn_kernel(qt_ref, k_ref, vt_ref, o_ref, acc_ref, sa_ref, sb_ref, mta_ref, mtb_ref):
    tq, tk = ATTN_TQ, ATTN_TK
    qi = pl.program_id(2)
    acc_ref[...] = jnp.zeros_like(acc_ref)

    def produce(j, s_ref, mt_ref):
        off = pl.multiple_of(j * tk, tk)
        for h in range(ATTN_HEADS):
            s = jnp.dot(k_ref[0, h, pl.ds(off, tk), :], qt_ref[0, h], preferred_element_type=F32)
            s_ref[h] = s
            mt_ref[h, 0:1, :] = jnp.max(s, axis=0, keepdims=True)

    def consume(j, s_ref, mt_ref, ms, masked):
        new_ms = []
        for h in range(ATTN_HEADS):
            if masked:
                key_pos = lax.broadcasted_iota(jnp.int32, (tk, tq), 0) + j * tk
                qry_pos = lax.broadcasted_iota(jnp.int32, (tk, tq), 1) + qi * tq
                s = jnp.where(key_pos <= qry_pos, s_ref[h], -jnp.inf)
                m_tile = jnp.max(s, axis=0, keepdims=True)
            else:
                s = s_ref[h]
                m_tile = mt_ref[h, 0:1, :]
            m_new = jnp.maximum(ms[h], m_tile)
            alpha = jnp.exp(ms[h] - m_new)
            p = jnp.exp(s - m_new).astype(BF16)
            acc_ref[h] = alpha * acc_ref[h] + jnp.dot(vt_ref[0, h, j], p, preferred_element_type=F32)
            new_ms.append(m_new)
        return tuple(new_ms)

    def pair(jj, ms):
        produce(2 * jj + 1, sb_ref, mtb_ref)
        ms = consume(2 * jj, sa_ref, mta_ref, ms, False)
        produce(2 * jj + 2, sa_ref, mta_ref)
        return consume(2 * jj + 1, sb_ref, mtb_ref, ms, False)

    produce(0, sa_ref, mta_ref)
    m0 = jnp.full((1, tq), -jnp.inf, F32)
    ms = lax.fori_loop(0, qi, pair, (m0,) * ATTN_HEADS)
    produce(2 * qi + 1, sb_ref, mtb_ref)
    ms = consume(2 * qi, sa_ref, mta_ref, ms, True)
    consume(2 * qi + 1, sb_ref, mtb_ref, ms, True)

    outs = [(acc_ref[h] / acc_ref[h, HEAD_DIM:HEAD_DIM + 1, :])[0:HEAD_DIM] for h in range(ATTN_HEADS)]
    o_ref[0] = jnp.concatenate(outs, axis=0).T.astype(BF16)


def _causal_attn(qt, k, vt):
    bsz, n_heads, seq, _ = k.shape
    assert ATTN_TQ == 2 * ATTN_TK and seq % ATTN_TQ == 0 and n_heads % ATTN_HEADS == 0
    return pl.pallas_call(
        _attn_kernel,
        grid=(bsz, n_heads // ATTN_HEADS, seq // ATTN_TQ),
        in_specs=[
            pl.BlockSpec((1, ATTN_HEADS, HEAD_PAD, ATTN_TQ), lambda b, hp, i: (b, hp, 0, i)),
            pl.BlockSpec((1, ATTN_HEADS, seq, HEAD_PAD), lambda b, hp, i: (b, hp, 0, 0)),
            pl.BlockSpec((1, ATTN_HEADS, seq // ATTN_TK, V_ROWS, ATTN_TK), lambda b, hp, i: (b, hp, 0, 0, 0)),
        ],
        out_specs=pl.BlockSpec((1, ATTN_TQ, ATTN_HEADS * HEAD_DIM), lambda b, hp, i: (b, i, hp)),
        out_shape=jax.ShapeDtypeStruct((bsz, seq, n_heads * HEAD_DIM), BF16),
        scratch_shapes=[pltpu.VMEM((ATTN_HEADS, V_ROWS, ATTN_TQ), F32),
                        pltpu.VMEM((ATTN_HEADS, ATTN_TK, ATTN_TQ), F32),
                        pltpu.VMEM((ATTN_HEADS, ATTN_TK, ATTN_TQ), F32),
                        pltpu.VMEM((ATTN_HEADS, 8, ATTN_TQ), F32),
                        pltpu.VMEM((ATTN_HEADS, 8, ATTN_TQ), F32)],
        compiler_params=pltpu.CompilerParams(
            dimension_semantics=("parallel", "parallel", "arbitrary"), vmem_limit_bytes=VMEM_LIMIT),
        name="causal_attn",
    )(qt, k, vt)


def _out_proj_kernel(ya_ref, yb_ref, yc_ref, yd_ref, x_ref, w_ref, g_ref, b_ref, o_ref):
    y = jnp.concatenate([ya_ref[...], yb_ref[...], yc_ref[...], yd_ref[...]], axis=1)
    r = DN_ALPHA * x_ref[...] + jnp.dot(y, w_ref[...], preferred_element_type=F32)
    o_ref[...] = _layer_norm_rows(r, g_ref[...], b_ref[...])


def _out_proj_ln(ya, yb, yc, yd, x, layer, w_o, g, b):
    m = x.shape[0]
    rows = PROJ_ROWS
    assert m % rows == 0
    part = pl.BlockSpec((rows, GROUP_W), lambda i: (i, 0))
    vec = pl.BlockSpec((None, 1, D_MODEL), lambda i: (layer, 0, 0))
    return pl.pallas_call(
        _out_proj_kernel,
        grid=(m // rows,),
        in_specs=[part, part, part, part,
                  pl.BlockSpec((rows, D_MODEL), lambda i: (i, 0)),
                  pl.BlockSpec((None, D_MODEL, D_MODEL), lambda i: (layer, 0, 0)), vec, vec],
        out_specs=pl.BlockSpec((rows, D_MODEL), lambda i: (i, 0)),
        out_shape=jax.ShapeDtypeStruct((m, D_MODEL), F32),
        compiler_params=pltpu.CompilerParams(dimension_semantics=("parallel",), vmem_limit_bytes=VMEM_LIMIT),
        name="out_proj_ln",
    )(ya, yb, yc, yd, x, w_o, g, b)


def _ffn_kernel(tiles_per_seq, x_ref, xp_ref, wu_ref, cw_ref, cb_ref, wd_ref, g_ref, b_ref, o_ref,
                xb_ref, ha_ref, hb_ref, acc_ref):
    m = pl.program_id(0)
    cols = FFN_COLS
    n_col = D_FF // cols
    prev = jnp.where(m % tiles_per_seq == 0, 0.0, xp_ref[...])
    xb_ref[0:FFN_HALO, :] = prev.astype(BF16)
    xb_ref[FFN_HALO:, :] = x_ref[...].astype(BF16)

    def both(n, ref):
        return jnp.concatenate([ref[:, n * cols:(n + 1) * cols], ref[:, D_FF + n * cols:D_FF + (n + 1) * cols]],
                               axis=1)

    def up(n, h_ref):
        xb = xb_ref[...]
        h_ref[:, 0:cols] = jnp.dot(xb, wu_ref[:, n * cols:(n + 1) * cols], preferred_element_type=F32)
        h_ref[:, cols:2 * cols] = jnp.dot(xb, wu_ref[:, D_FF + n * cols:D_FF + (n + 1) * cols],
                                          preferred_element_type=F32)

    def gated(n, h_ref):
        h = h_ref[...]
        cw = both(n, cw_ref)
        c = (pltpu.roll(h, 2, 0) * cw[0:1, :] + pltpu.roll(h, 1, 0) * cw[1:2, :] + h * cw[2:3, :]
             + both(n, cb_ref))[FFN_HALO:, :]
        gate, val = c[:, 0:cols], c[:, cols:2 * cols]
        return (gate * jax.nn.sigmoid(gate) * val).astype(BF16)

    def down(n, act):
        return jnp.dot(act, wd_ref[n * cols:(n + 1) * cols, :], preferred_element_type=F32)

    bufs = (ha_ref, hb_ref)
    up(0, bufs[0])
    parts = []
    for n in range(n_col):
        if n + 1 < n_col:
            up(n + 1, bufs[(n + 1) % 2])
        parts.append(down(n, gated(n, bufs[n % 2])))
        if len(parts) == 2:
            if n == 1:
                acc_ref[...] = parts[0] + parts[1]
            else:
                acc_ref[...] += parts[0] + parts[1]
            parts = []
    y = acc_ref[...] + parts[0]
    o_ref[...] = _layer_norm_rows(DN_ALPHA * x_ref[...] + y, g_ref[...], b_ref[...])


def _conv_ffn_ln(x, seq, layer, w_up, conv_w, conv_b, w_down, g, b):
    m = x.shape[0]
    rows = FFN_ROWS
    n_col = D_FF // FFN_COLS
    assert m % rows == 0 and seq % rows == 0 and rows % FFN_HALO == 0 and n_col % 2 == 1 and n_col >= 3
    halo_blocks = rows // FFN_HALO
    resident = lambda *shape: pl.BlockSpec((None,) + shape, lambda i: (layer,) + (0,) * len(shape),
                                           pipeline_mode=pl.Buffered(1))
    return pl.pallas_call(
        functools.partial(_ffn_kernel, seq // rows),
        grid=(m // rows,),
        in_specs=[
            pl.BlockSpec((rows, D_MODEL), lambda i: (i, 0)),
            pl.BlockSpec((FFN_HALO, D_MODEL), lambda i: (jnp.maximum(i * halo_blocks - 1, 0), 0)),
            resident(D_MODEL, 2 * D_FF),
            resident(3, 2 * D_FF),
            resident(1, 2 * D_FF),
            resident(D_FF, D_MODEL),
            resident(1, D_MODEL), resident(1, D_MODEL),
        ],
        out_specs=pl.BlockSpec((rows, D_MODEL), lambda i: (i, 0)),
        out_shape=jax.ShapeDtypeStruct((m, D_MODEL), F32),
        scratch_shapes=[pltpu.VMEM((FFN_HALO + rows, D_MODEL), BF16),
                        pltpu.VMEM((FFN_HALO + rows, 2 * FFN_COLS), F32),
                        pltpu.VMEM((FFN_HALO + rows, 2 * FFN_COLS), F32),
                        pltpu.VMEM((rows, D_MODEL), F32)],
        compiler_params=pltpu.CompilerParams(dimension_semantics=("parallel",), vmem_limit_bytes=VMEM_LIMIT),
        name="conv_ffn_ln",
    )(x, x, w_up, conv_w, conv_b, w_down, g, b)


def _rotary_lane_tables(seq):
    pos = jnp.arange(seq, dtype=F32)
    inv_freq = ROPE_THETA ** (-jnp.arange(0, ROPE_DIM, 2, dtype=F32) / ROPE_DIM)
    ang = inv_freq[:, None] * pos[None, :]
    cos, sin = jnp.cos(ang), jnp.sin(ang)
    half = ROPE_DIM // 2
    ones = jnp.ones((HEAD_DIM - ROPE_DIM, seq), F32)
    zeros = lambda n: jnp.zeros((n, seq), F32)
    cos_t = jnp.concatenate([cos, cos, ones], axis=0)
    sin_a = jnp.concatenate([-sin, zeros(HEAD_DIM - half)], axis=0)
    sin_b = jnp.concatenate([zeros(half), sin, zeros(HEAD_DIM - ROPE_DIM)], axis=0)
    tables = jnp.stack([jnp.concatenate([t, t], axis=0) for t in (cos_t, sin_a, sin_b)])
    return tables.transpose(0, 2, 1)


def kernel(x, w_in, b_forget, sgu_ln_g, sgu_ln_b, sgu_w, sgu_b, pool_w, pool_scale, w_o, ln1_g, ln1_b,
           w_up, conv_w, conv_b, w_down, ln2_g, ln2_b):
    bsz, seq, _ = x.shape
    depth = w_in.shape[0]
    row = lambda t: t.reshape(depth, 1, t.shape[-1])
    rot = _rotary_lane_tables(seq)
    group = jnp.arange(GROUP_W) // HEAD_DIM
    avg = jnp.where(group[:, None] == group[None, :], 1.0 / HEAD_DIM, 0.0).astype(BF16)
    w_in_b = w_in.astype(BF16)
    f0 = COL_FORGET
    w_rest = jnp.concatenate([w_in[:, :, f0 + N_HEADS:], w_in[:, :, f0:f0 + N_HEADS],
                              jnp.zeros((depth, D_MODEL, 128 - N_HEADS), F32)], axis=-1).astype(BF16)
    bfg = jnp.pad(b_forget, ((0, 0), (0, 128 - N_HEADS))).reshape(depth, 1, 128)
    sgu_bias = jnp.repeat(sgu_b.transpose(0, 2, 1), HEAD_DIM, axis=2)
    n_pool = len(POOL_WINDOWS)
    same_group = jnp.eye(n_pool, dtype=bool)[None, :, None, :, None]
    pool_bd = jnp.where(same_group, pool_w[:, :, :, None, :], 0.0).reshape(
        depth, GROUP_W, GROUP_W).astype(BF16)
    w_o_b, w_up_b, w_down_b = w_o.astype(BF16), w_up.astype(BF16), w_down.astype(BF16)

    xf = x.reshape(bsz * seq, D_MODEL)
    for l in range(depth):
        qa, ka, va, qb, kb, vb, yc, yd = _mixer_front(
            xf.reshape(bsz, seq, D_MODEL), l, w_in_b, w_rest, rot, bfg, row(sgu_ln_g), row(sgu_ln_b), sgu_w,
            sgu_bias, avg, pool_bd, row(pool_scale))
        ya = _causal_attn(qa, ka, va).reshape(bsz * seq, GROUP_W)
        yb = _causal_attn(qb, kb, vb).reshape(bsz * seq, GROUP_W)
        xf = _out_proj_ln(ya, yb, yc.reshape(bsz * seq, GROUP_W), yd.reshape(bsz * seq, GROUP_W), xf, l,
                          w_o_b, row(ln1_g), row(ln1_b))
        xf = _conv_ffn_ln(xf, seq, l, w_up_b, conv_w, row(conv_b), w_down_b, row(ln2_g), row(ln2_b))
    return xf.reshape(bsz, seq, D_MODEL)
```

```python
import functools

import numpy as np
import jax
import jax.numpy as jnp
from jax import lax
from jax.experimental import pallas as pl
from jax.experimental.pallas import tpu as pltpu

F32 = jnp.float32
BF16 = jnp.bfloat16

D_MODEL = 1024
DEPTH = 2
HEAD_DIM = 64
N_HEADS = 4
GROUP_W = 256
HEAD_PAD = 128
PAIR_W = 2 * HEAD_DIM
V_ROWS = 80
MOBA_BLOCK = 256
MOBA_TOPK = 3
ROPE_THETA = 500000.0
ROPE_DIM = HEAD_DIM // 4
SGU_CHUNK = 128
SGU_GROUPS = 4
POOL_WINDOWS = (2, 4, 8, 16)
POOL_HALO = 16
D_FF = 2816
DN_ALPHA = (2 * DEPTH) ** 0.25
LN_EPS = 1e-5
LOG2_E = 1.4426950408889634
QK_SCALE = HEAD_DIM ** -0.5 * LOG2_E
MASK_BIAS = -1e30

FRONT_ROWS = MOBA_BLOCK
ATTN_TQ = 512
ATTN_TK = 256
ATTN_HEADS = 4
PROJ_ROWS = 1024
FFN_ROWS = 512
FFN_COLS = 256
FFN_HALO = 16
VMEM_LIMIT = 56 * 1024 * 1024

COL_A = 0
COL_B = 3 * GROUP_W
COL_FORGET = 6 * GROUP_W
REST_W = 3 * GROUP_W + 128


def _layer_norm_rows(r, g, b):
    mu = jnp.mean(r, axis=-1, keepdims=True)
    d = r - mu
    var = jnp.mean(d * d, axis=-1, keepdims=True)
    return d * lax.rsqrt(var + LN_EPS) * g + b


def _split_bf16(t):
    hi = t.astype(BF16)
    lo = (t - hi.astype(F32)).astype(BF16)
    return hi, lo


def _group_mean(t, avg):
    hi, lo = _split_bf16(t)
    return (jnp.dot(hi, avg, preferred_element_type=F32) + jnp.dot(lo, avg, preferred_element_type=F32))


def _front_kernel(x_ref, wqkv_ref, wrest_ref, rot_ref, bfg_ref, lng_ref, lnb_ref, sw_ref, sb_ref,
                  avg_ref, pw_ref, ps_ref,
                  qa_ref, ka_ref, va_ref, qb_ref, kb_ref, vb_ref, yc_ref, yd_ref,
                  kbar_ref, carry_ref, halo_ref):
    rows = FRONT_ROWS
    i = pl.program_id(1)
    n_blk = kbar_ref.shape[1]

    @pl.when(i == 0)
    def _():
        kbar_ref[...] = jnp.zeros_like(kbar_ref)
        carry_ref[...] = jnp.zeros_like(carry_ref)
        halo_ref[...] = jnp.zeros_like(halo_ref)

    xb = x_ref[0].astype(BF16)
    lane = lax.broadcasted_iota(jnp.int32, (rows, PAIR_W), 1)
    ones_slab = jnp.where(lax.broadcasted_iota(jnp.int32, (V_ROWS - HEAD_DIM, rows), 0) == 0, 1.0, 0.0)

    def head_rows(pair_t, e):
        return pair_t[e * HEAD_DIM:(e + 1) * HEAD_DIM]

    def head_lanes_first(pair, e):
        return pair if e == 0 else pltpu.roll(pair, HEAD_DIM, 1)

    za = jnp.dot(xb, wqkv_ref[:, COL_A:COL_A + 3 * GROUP_W], preferred_element_type=F32)
    cos_t, sin_a, sin_b = rot_ref[0], rot_ref[1], rot_ref[2]

    def rotary(t):
        return (t * cos_t + pltpu.roll(t, PAIR_W - ROPE_DIM // 2, 1) * sin_a
                + pltpu.roll(t, ROPE_DIM // 2, 1) * sin_b)

    blk_row = lax.broadcasted_iota(jnp.int32, (n_blk, rows), 0)
    blk_lane = lax.broadcasted_iota(jnp.int32, (n_blk, PAIR_W), 1)
    for pr in range(N_HEADS // 2):
        q2 = rotary(za[:, pr * PAIR_W:(pr + 1) * PAIR_W])
        k2 = rotary(za[:, GROUP_W + pr * PAIR_W:GROUP_W + (pr + 1) * PAIR_W])
        q2_t = (q2 * QK_SCALE).T
        v2_t = za[:, 2 * GROUP_W + pr * PAIR_W:2 * GROUP_W + (pr + 1) * PAIR_W].T
        for e in range(2):
            h = 2 * pr + e
            kbar = jnp.where(blk_lane // HEAD_DIM == e, kbar_ref[pr], 0.0)
            gate_t = lax.dot_general(kbar, q2, (((1,), (1,)), ((), ())),
                                     precision=lax.Precision.HIGHEST, preferred_element_type=F32)
            g = jnp.where(blk_row < i, gate_t, -jnp.inf)
            chosen = jnp.zeros((n_blk, rows), F32)
            for _ in range(MOBA_TOPK):
                top = jnp.max(g, axis=0, keepdims=True)
                first = jnp.min(jnp.where(g == top, blk_row, n_blk), axis=0, keepdims=True)
                pick = blk_row == first
                chosen = jnp.where(pick, 1.0, chosen)
                g = jnp.where(pick, -jnp.inf, g)
            keep = jnp.where(blk_row < i, chosen, jnp.where(blk_row == i, 1.0, 0.0))
            bias_t = jnp.where(keep > 0.0, 0.0, MASK_BIAS)
            qa_ref[0, h] = jnp.concatenate(
                [head_rows(q2_t, e), bias_t, jnp.zeros((HEAD_PAD - HEAD_DIM - n_blk, rows), F32)],
                axis=0).astype(BF16)
            ka_ref[0, h] = jnp.where(lane < HEAD_DIM, head_lanes_first(k2, e),
                                     jnp.where(lane == HEAD_DIM + i, 1.0, 0.0)).astype(BF16)
            va_ref[0, h, 0] = jnp.concatenate([head_rows(v2_t, e), ones_slab], axis=0).astype(BF16)
        kbar_ref[pr, pl.ds(i, 1), :] = jnp.sum(k2, axis=0, keepdims=True) * (1.0 / MOBA_BLOCK)

    zr = jnp.dot(xb, wrest_ref[...], preferred_element_type=F32)
    cu = zr[:, 0:GROUP_W]
    cv = zr[:, GROUP_W:2 * GROUP_W]
    dp = zr[:, 2 * GROUP_W:3 * GROUP_W]
    fl = zr[:, 3 * GROUP_W:3 * GROUP_W + 128] + bfg_ref[...]
    log_f = jnp.minimum(fl, 0.0) - jnp.log1p(jnp.exp(-jnp.abs(fl)))
    log_f = jnp.where(lane < N_HEADS, log_f, 0.0)
    r_i = lax.broadcasted_iota(jnp.int32, (rows, rows), 0)
    c_i = lax.broadcasted_iota(jnp.int32, (rows, rows), 1)
    tri = jnp.where(c_i <= r_i, 1.0, 0.0).astype(BF16)
    f_hi = log_f.astype(BF16)
    f_res = log_f - f_hi.astype(F32)
    f_mid = f_res.astype(BF16)
    f_lo = (f_res - f_mid.astype(F32)).astype(BF16)
    cum = (jnp.dot(tri, f_hi, preferred_element_type=F32) + jnp.dot(tri, f_mid, preferred_element_type=F32)
           + jnp.dot(tri, f_lo, preferred_element_type=F32) + carry_ref[0:1, :])
    carry_ref[0:1, :] = cum[rows - 1:rows, :]

    cum = cum * LOG2_E
    zb = jnp.dot(xb, wqkv_ref[:, COL_B:COL_B + 3 * GROUP_W], preferred_element_type=F32)
    cum_t = cum.T
    sub8 = lax.broadcasted_iota(jnp.int32, (8, rows), 0)
    for pr in range(N_HEADS // 2):
        q2_t = (zb[:, pr * PAIR_W:(pr + 1) * PAIR_W] * QK_SCALE).T
        k2 = zb[:, GROUP_W + pr * PAIR_W:GROUP_W + (pr + 1) * PAIR_W]
        v2_t = zb[:, 2 * GROUP_W + pr * PAIR_W:2 * GROUP_W + (pr + 1) * PAIR_W].T
        for e in range(2):
            h = 2 * pr + e
            c = jnp.broadcast_to(cum[:, h:h + 1], (rows, PAIR_W))
            hi = c.astype(BF16).astype(F32)
            mid = (c - hi).astype(BF16).astype(F32)
            lo = c - hi - mid
            k_aug = jnp.where(lane < HEAD_DIM, head_lanes_first(k2, e),
                    jnp.where(lane < HEAD_DIM + 3, 1.0,
                    jnp.where(lane == HEAD_DIM + 3, -hi,
                    jnp.where(lane == HEAD_DIM + 4, -mid,
                    jnp.where(lane == HEAD_DIM + 5, -lo, 0.0)))))
            c_t = jnp.broadcast_to(cum_t[h:h + 1, :], (8, rows))
            hi_t = c_t.astype(BF16).astype(F32)
            mid_t = (c_t - hi_t).astype(BF16).astype(F32)
            lo_t = c_t - hi_t - mid_t
            q_bias = jnp.where(sub8 == 0, hi_t, jnp.where(sub8 == 1, mid_t, jnp.where(sub8 == 2, lo_t,
                     jnp.where(sub8 < 6, 1.0, 0.0))))
            qb_ref[0, h] = jnp.concatenate(
                [head_rows(q2_t, e), q_bias, jnp.zeros((HEAD_PAD - HEAD_DIM - 8, rows), F32)],
                axis=0).astype(BF16)
            kb_ref[0, h] = k_aug.astype(BF16)
            vb_ref[0, h, 0] = jnp.concatenate([head_rows(v2_t, e), ones_slab], axis=0).astype(BF16)

    inv_sqrt2 = np.float32(1.0 / np.sqrt(2.0))
    u = 0.5 * cu * (1.0 + lax.erf(cu * inv_sqrt2))
    vg = 0.5 * cv * (1.0 + lax.erf(cv * inv_sqrt2))
    avg = avg_ref[...]
    mu = _group_mean(vg, avg)
    dv = vg - mu
    var = _group_mean(dv * dv, avg)
    vn = dv * lax.rsqrt(var + LN_EPS) * lng_ref[...] + lnb_ref[...]
    lane_w = lax.broadcasted_iota(jnp.int32, (SGU_CHUNK, GROUP_W), 1)
    t_i = lax.broadcasted_iota(jnp.int32, (SGU_CHUNK, SGU_CHUNK), 0)
    s_i = lax.broadcasted_iota(jnp.int32, (SGU_CHUNK, SGU_CHUNK), 1)
    w_tril = [jnp.where(s_i <= t_i, sw_ref[g], 0.0).astype(BF16) for g in range(SGU_GROUPS)]
    for c in range(rows // SGU_CHUNK):
        vn_c = vn[c * SGU_CHUNK:(c + 1) * SGU_CHUNK, :].astype(BF16)
        mixed = sb_ref[...]
        for g in range(SGU_GROUPS):
            mg = jnp.dot(w_tril[g], vn_c, preferred_element_type=F32)
            mixed = mixed + jnp.where(lane_w // HEAD_DIM == g, mg, 0.0)
        yc_ref[0, c * SGU_CHUNK:(c + 1) * SGU_CHUNK, :] = (
            u[c * SGU_CHUNK:(c + 1) * SGU_CHUNK, :] * mixed).astype(BF16)

    ext = jnp.concatenate([halo_ref[...], dp], axis=0)
    s2 = ext + pltpu.roll(ext, 1, 0)
    s4 = s2 + pltpu.roll(s2, 2, 0)
    s8 = s4 + pltpu.roll(s4, 4, 0)
    s16 = s8 + pltpu.roll(s8, 8, 0)
    lane_g = lax.broadcasted_iota(jnp.int32, (rows, GROUP_W), 1) // (GROUP_W // len(POOL_WINDOWS))
    t_glob = lax.broadcasted_iota(jnp.int32, (rows, GROUP_W), 0) + i * rows
    win = jnp.where(lane_g == 0, POOL_WINDOWS[0], jnp.where(lane_g == 1, POOL_WINDOWS[1],
          jnp.where(lane_g == 2, POOL_WINDOWS[2], POOL_WINDOWS[3])))
    wsum = jnp.where(lane_g == 0, s2[POOL_HALO:], jnp.where(lane_g == 1, s4[POOL_HALO:],
           jnp.where(lane_g == 2, s8[POOL_HALO:], s16[POOL_HALO:])))
    count = jnp.minimum(t_glob + 1, win).astype(F32)
    pooled = wsum / count - dp
    yd = jnp.dot(pooled.astype(BF16), pw_ref[...], preferred_element_type=F32) * ps_ref[...]
    yd_ref[0] = yd.astype(BF16)
    halo_ref[...] = dp[rows - POOL_HALO:, :]


def _mixer_front(x, layer, w_in, w_rest, rot, bfg, lng, lnb, sgu_w, sgu_bias, avg, pool_w, pool_s):
    bsz, seq, _ = x.shape
    n_blk = seq // MOBA_BLOCK
    assert seq % ATTN_TK == 0 and ATTN_TK % FRONT_ROWS == 0
    assert n_blk <= HEAD_PAD - HEAD_DIM and n_blk % 8 == 0
    rows = FRONT_ROWS
    const = lambda *shape: pl.BlockSpec(shape, lambda b, i: (0,) * len(shape))
    of_layer = lambda *shape: pl.BlockSpec((None,) + shape, lambda b, i: (layer,) + (0,) * len(shape))
    per_tk = ATTN_TK // rows
    q_spec = pl.BlockSpec((1, N_HEADS, HEAD_PAD, rows), lambda b, i: (b, 0, 0, i))
    k_spec = pl.BlockSpec((1, N_HEADS, rows, HEAD_PAD), lambda b, i: (b, 0, i, 0))
    v_spec = pl.BlockSpec((1, N_HEADS, 1, V_ROWS, rows), lambda b, i: (b, 0, i // per_tk, 0, i % per_tk))
    flat_spec = pl.BlockSpec((1, rows, GROUP_W), lambda b, i: (b, i, 0))
    q_shape = jax.ShapeDtypeStruct((bsz, N_HEADS, HEAD_PAD, seq), BF16)
    k_shape = jax.ShapeDtypeStruct((bsz, N_HEADS, seq, HEAD_PAD), BF16)
    v_shape = jax.ShapeDtypeStruct((bsz, N_HEADS, seq // ATTN_TK, V_ROWS, ATTN_TK), BF16)
    flat_shape = jax.ShapeDtypeStruct((bsz, seq, GROUP_W), BF16)
    return pl.pallas_call(
        _front_kernel,
        grid=(bsz, seq // rows),
        in_specs=[
            pl.BlockSpec((1, rows, D_MODEL), lambda b, i: (b, i, 0)),
            of_layer(D_MODEL, COL_FORGET),
            of_layer(D_MODEL, REST_W),
            pl.BlockSpec((3, rows, PAIR_W), lambda b, i: (0, i, 0)),
            of_layer(1, 128), of_layer(1, GROUP_W), of_layer(1, GROUP_W),
            of_layer(SGU_GROUPS, SGU_CHUNK, SGU_CHUNK), of_layer(SGU_CHUNK, GROUP_W),
            const(GROUP_W, GROUP_W), of_layer(GROUP_W, GROUP_W), of_layer(1, GROUP_W),
        ],
        out_specs=[q_spec, k_spec, v_spec] * 2 + [flat_spec] * 2,
        out_shape=[q_shape, k_shape, v_shape] * 2 + [flat_shape] * 2,
        scratch_shapes=[
            pltpu.VMEM((N_HEADS // 2, n_blk, PAIR_W), F32),
            pltpu.VMEM((8, 128), F32),
            pltpu.VMEM((POOL_HALO, GROUP_W), F32),
        ],
        compiler_params=pltpu.CompilerParams(
            dimension_semantics=("arbitrary", "arbitrary"), vmem_limit_bytes=VMEM_LIMIT),
        name="mixer_front",
    )(x, w_in, w_rest, rot, bfg, lng, lnb, sgu_w, sgu_bias, avg, pool_w, pool_s)


def _attn_kernel(qt_ref, k_ref, vt_ref, o_ref, acc_ref, sa_ref, sb_ref, mta_ref, mtb_ref):
    tq, tk = ATTN_TQ, ATTN_TK
    qi = pl.program_id(2)
    acc_ref[...] = jnp.zeros_like(acc_ref)

    def produce(j, s_ref, mt_ref):
        off = pl.multiple_of(j * tk, tk)
        for h in range(ATTN_HEADS):
            s = jnp.dot(k_ref[0, h, pl.ds(off, tk), :], qt_ref[0, h], preferred_element_type=F32)
            s_ref[h] = s
            mt_ref[h, 0:1, :] = jnp.max(s, axis=0, keepdims=True)

    def consume(j, s_ref, mt_ref, ms, masked):
        new_ms = []
        for h in range(ATTN_HEADS):
            if masked:
                key_pos = lax.broadcasted_iota(jnp.int32, (tk, tq), 0) + j * tk
                qry_pos = lax.broadcasted_iota(jnp.int32, (tk, tq), 1) + qi * tq
                s = jnp.where(key_pos <= qry_pos, s_ref[h], -jnp.inf)
                m_tile = jnp.max(s, axis=0, keepdims=True)
            else:
                s = s_ref[h]
                m_tile = mt_ref[h, 0:1, :]
            m_new = jnp.maximum(ms[h], m_tile)
            alpha = jnp.exp2(ms[h] - m_new)
            p = jnp.exp2(s - m_new).astype(BF16)
            acc_ref[h] = alpha * acc_ref[h] + jnp.dot(vt_ref[0, h, j], p, preferred_element_type=F32)
            new_ms.append(m_new)
        return tuple(new_ms)

    def pair(jj, ms):
        produce(2 * jj + 1, sb_ref, mtb_ref)
        ms = consume(2 * jj, sa_ref, mta_ref, ms, False)
        produce(2 * jj + 2, sa_ref, mta_ref)
        return consume(2 * jj + 1, sb_ref, mtb_ref, ms, False)

    produce(0, sa_ref, mta_ref)
    m0 = jnp.full((1, tq), -jnp.inf, F32)
    ms = lax.fori_loop(0, qi, pair, (m0,) * ATTN_HEADS)
    produce(2 * qi + 1, sb_ref, mtb_ref)
    ms = consume(2 * qi, sa_ref, mta_ref, ms, True)
    consume(2 * qi + 1, sb_ref, mtb_ref, ms, True)

    outs = [(acc_ref[h] / acc_ref[h, HEAD_DIM:HEAD_DIM + 1, :])[0:HEAD_DIM] for h in range(ATTN_HEADS)]
    o_ref[0] = jnp.concatenate(outs, axis=0).T.astype(BF16)


def _causal_attn(qt, k, vt):
    bsz, n_heads, seq, _ = k.shape
    assert ATTN_TQ == 2 * ATTN_TK and seq % ATTN_TQ == 0 and n_heads % ATTN_HEADS == 0
    return pl.pallas_call(
        _attn_kernel,
        grid=(bsz, n_heads // ATTN_HEADS, seq // ATTN_TQ),
        in_specs=[
            pl.BlockSpec((1, ATTN_HEADS, HEAD_PAD, ATTN_TQ), lambda b, hp, i: (b, hp, 0, i)),
            pl.BlockSpec((1, ATTN_HEADS, seq, HEAD_PAD), lambda b, hp, i: (b, hp, 0, 0)),
            pl.BlockSpec((1, ATTN_HEADS, seq // ATTN_TK, V_ROWS, ATTN_TK), lambda b, hp, i: (b, hp, 0, 0, 0)),
        ],
        out_specs=pl.BlockSpec((1, ATTN_TQ, ATTN_HEADS * HEAD_DIM), lambda b, hp, i: (b, i, hp)),
        out_shape=jax.ShapeDtypeStruct((bsz, seq, n_heads * HEAD_DIM), BF16),
        scratch_shapes=[pltpu.VMEM((ATTN_HEADS, V_ROWS, ATTN_TQ), F32),
                        pltpu.VMEM((ATTN_HEADS, ATTN_TK, ATTN_TQ), F32),
                        pltpu.VMEM((ATTN_HEADS, ATTN_TK, ATTN_TQ), F32),
                        pltpu.VMEM((ATTN_HEADS, 8, ATTN_TQ), F32),
                        pltpu.VMEM((ATTN_HEADS, 8, ATTN_TQ), F32)],
        compiler_params=pltpu.CompilerParams(
            dimension_semantics=("parallel", "parallel", "arbitrary"), vmem_limit_bytes=VMEM_LIMIT),
        name="causal_attn",
    )(qt, k, vt)


def _out_proj_kernel(ya_ref, yb_ref, yc_ref, yd_ref, x_ref, w_ref, g_ref, b_ref, o_ref):
    half = PROJ_ROWS // 2
    spans = [slice(i * half, (i + 1) * half) for i in range(2)]
    ys = []
    for sp in spans:
        y = jnp.concatenate([ya_ref[sp, :], yb_ref[sp, :], yc_ref[sp, :], yd_ref[sp, :]], axis=1)
        ys.append(jnp.dot(y, w_ref[...], preferred_element_type=F32))
    for sp, y in zip(spans, ys):
        o_ref[sp, :] = _layer_norm_rows(DN_ALPHA * x_ref[sp, :] + y, g_ref[...], b_ref[...])


def _out_proj_ln(ya, yb, yc, yd, x, layer, w_o, g, b):
    m = x.shape[0]
    rows = PROJ_ROWS
    assert m % rows == 0
    part = pl.BlockSpec((rows, GROUP_W), lambda i: (i, 0))
    vec = pl.BlockSpec((None, 1, D_MODEL), lambda i: (layer, 0, 0))
    return pl.pallas_call(
        _out_proj_kernel,
        grid=(m // rows,),
        in_specs=[part, part, part, part,
                  pl.BlockSpec((rows, D_MODEL), lambda i: (i, 0)),
                  pl.BlockSpec((None, D_MODEL, D_MODEL), lambda i: (layer, 0, 0)), vec, vec],
        out_specs=pl.BlockSpec((rows, D_MODEL), lambda i: (i, 0)),
        out_shape=jax.ShapeDtypeStruct((m, D_MODEL), F32),
        compiler_params=pltpu.CompilerParams(dimension_semantics=("parallel",), vmem_limit_bytes=VMEM_LIMIT),
        name="out_proj_ln",
    )(ya, yb, yc, yd, x, w_o, g, b)


def _ffn_kernel(tiles_per_seq, x_ref, xp_ref, wu_ref, cw_ref, cb_ref, wd_ref, g_ref, b_ref, o_ref,
                xb_ref, ha_ref, hb_ref, acc_ref):
    m = pl.program_id(0)
    cols = FFN_COLS
    n_col = D_FF // cols
    prev = jnp.where(m % tiles_per_seq == 0, 0.0, xp_ref[...])
    xb_ref[0:FFN_HALO, :] = prev.astype(BF16)
    xb_ref[FFN_HALO:, :] = x_ref[...].astype(BF16)

    def both(n, ref):
        return jnp.concatenate([ref[:, n * cols:(n + 1) * cols], ref[:, D_FF + n * cols:D_FF + (n + 1) * cols]],
                               axis=1)

    def up(n, h_ref):
        xb = xb_ref[...]
        h_ref[:, 0:cols] = jnp.dot(xb, wu_ref[:, n * cols:(n + 1) * cols], preferred_element_type=F32)
        h_ref[:, cols:2 * cols] = jnp.dot(xb, wu_ref[:, D_FF + n * cols:D_FF + (n + 1) * cols],
                                          preferred_element_type=F32)

    def gated(n, h_ref):
        h = h_ref[...]
        cw = both(n, cw_ref)
        c = (pltpu.roll(h, 2, 0) * cw[0:1, :] + pltpu.roll(h, 1, 0) * cw[1:2, :] + h * cw[2:3, :]
             + both(n, cb_ref))[FFN_HALO:, :]
        gate, val = c[:, 0:cols], c[:, cols:2 * cols]
        return (gate * jax.nn.sigmoid(gate) * val).astype(BF16)

    def down(n, act):
        return jnp.dot(act, wd_ref[n * cols:(n + 1) * cols, :], preferred_element_type=F32)

    bufs = (ha_ref, hb_ref)
    up(0, bufs[0])
    parts = []
    for n in range(n_col):
        if n + 1 < n_col:
            up(n + 1, bufs[(n + 1) % 2])
        parts.append(down(n, gated(n, bufs[n % 2])))
        if len(parts) == 2:
            if n == 1:
                acc_ref[...] = parts[0] + parts[1]
            else:
                acc_ref[...] += parts[0] + parts[1]
            parts = []
    y = acc_ref[...] + parts[0]
    o_ref[...] = _layer_norm_rows(DN_ALPHA * x_ref[...] + y, g_ref[...], b_ref[...])


def _conv_ffn_ln(x, seq, layer, w_up, conv_w, conv_b, w_down, g, b):
    m = x.shape[0]
    rows = FFN_ROWS
    n_col = D_FF // FFN_COLS
    assert m % rows == 0 and seq % rows == 0 and rows % FFN_HALO == 0 and n_col % 2 == 1 and n_col >= 3
    halo_blocks = rows // FFN_HALO
    resident = lambda *shape: pl.BlockSpec((None,) + shape, lambda i: (layer,) + (0,) * len(shape),
                                           pipeline_mode=pl.Buffered(1))
    return pl.pallas_call(
        functools.partial(_ffn_kernel, seq // rows),
        grid=(m // rows,),
        in_specs=[
            pl.BlockSpec((rows, D_MODEL), lambda i: (i, 0)),
            pl.BlockSpec((FFN_HALO, D_MODEL), lambda i: (jnp.maximum(i * halo_blocks - 1, 0), 0)),
            resident(D_MODEL, 2 * D_FF),
            resident(3, 2 * D_FF),
            resident(1, 2 * D_FF),
            resident(D_FF, D_MODEL),
            resident(1, D_MODEL), resident(1, D_MODEL),
        ],
        out_specs=pl.BlockSpec((rows, D_MODEL), lambda i: (i, 0)),
        out_shape=jax.ShapeDtypeStruct((m, D_MODEL), F32),
        scratch_shapes=[pltpu.VMEM((FFN_HALO + rows, D_MODEL), BF16),
                        pltpu.VMEM((FFN_HALO + rows, 2 * FFN_COLS), F32),
                        pltpu.VMEM((FFN_HALO + rows, 2 * FFN_COLS), F32),
                        pltpu.VMEM((rows, D_MODEL), F32)],
        compiler_params=pltpu.CompilerParams(dimension_semantics=("parallel",), vmem_limit_bytes=VMEM_LIMIT),
        name="conv_ffn_ln",
    )(x, x, w_up, conv_w, conv_b, w_down, g, b)


def _rotary_lane_tables(seq):
    pos = jnp.arange(seq, dtype=F32)
    inv_freq = ROPE_THETA ** (-jnp.arange(0, ROPE_DIM, 2, dtype=F32) / ROPE_DIM)
    half = ROPE_DIM // 2
    d = np.arange(PAIR_W) % HEAD_DIM
    freq_lane = jnp.where(d < ROPE_DIM, inv_freq[d % half], 0.0)
    ang = pos[:, None] * freq_lane[None, :]
    cos, sin = jnp.cos(ang), jnp.sin(ang)
    sin_a = jnp.where(d < half, -sin, 0.0)
    sin_b = jnp.where((d >= half) & (d < ROPE_DIM), sin, 0.0)
    return jnp.stack([cos, sin_a, sin_b])


def kernel(x, w_in, b_forget, sgu_ln_g, sgu_ln_b, sgu_w, sgu_b, pool_w, pool_scale, w_o, ln1_g, ln1_b,
           w_up, conv_w, conv_b, w_down, ln2_g, ln2_b):
    bsz, seq, _ = x.shape
    depth = w_in.shape[0]
    row = lambda t: t.reshape(depth, 1, t.shape[-1])
    rot = _rotary_lane_tables(seq)
    group = jnp.arange(GROUP_W) // HEAD_DIM
    avg = jnp.where(group[:, None] == group[None, :], 1.0 / HEAD_DIM, 0.0).astype(BF16)
    w_in_b = w_in.astype(BF16)
    f0 = COL_FORGET
    w_rest = jnp.concatenate([w_in[:, :, f0 + N_HEADS:], w_in[:, :, f0:f0 + N_HEADS],
                              jnp.zeros((depth, D_MODEL, 128 - N_HEADS), F32)], axis=-1).astype(BF16)
    bfg = jnp.pad(b_forget, ((0, 0), (0, 128 - N_HEADS))).reshape(depth, 1, 128)
    sgu_bias = jnp.repeat(sgu_b.transpose(0, 2, 1), HEAD_DIM, axis=2)
    n_pool = len(POOL_WINDOWS)
    same_group = jnp.eye(n_pool, dtype=bool)[None, :, None, :, None]
    pool_bd = jnp.where(same_group, pool_w[:, :, :, None, :], 0.0).reshape(
        depth, GROUP_W, GROUP_W).astype(BF16)
    w_o_b, w_up_b, w_down_b = w_o.astype(BF16), w_up.astype(BF16), w_down.astype(BF16)

    xf = x.reshape(bsz * seq, D_MODEL)
    for l in range(depth):
        qa, ka, va, qb, kb, vb, yc, yd = _mixer_front(
            xf.reshape(bsz, seq, D_MODEL), l, w_in_b, w_rest, rot, bfg, row(sgu_ln_g), row(sgu_ln_b), sgu_w,
            sgu_bias, avg, pool_bd, row(pool_scale))
        ya = _causal_attn(qa, ka, va).reshape(bsz * seq, GROUP_W)
        yb = _causal_attn(qb, kb, vb).reshape(bsz * seq, GROUP_W)
        xf = _out_proj_ln(ya, yb, yc.reshape(bsz * seq, GROUP_W), yd.reshape(bsz * seq, GROUP_W), xf, l,
                          w_o_b, row(ln1_g), row(ln1_b))
        xf = _conv_ffn_ln(xf, seq, l, w_up_b, conv_w, row(conv_b), w_down_b, row(ln2_g), row(ln2_b))
    return xf.reshape(bsz, seq, D_MODEL)
```

```python
import functools

import numpy as np
import jax
import jax.numpy as jnp
from jax import lax
from jax.experimental import pallas as pl
from jax.experimental.pallas import tpu as pltpu

F32 = jnp.float32
BF16 = jnp.bfloat16

D_MODEL = 1024
DEPTH = 2
HEAD_DIM = 64
N_HEADS = 4
GROUP_W = 256
HEAD_PAD = 128
PAIR_W = 2 * HEAD_DIM
V_ROWS = 80
MOBA_BLOCK = 256
MOBA_TOPK = 3
ROPE_THETA = 500000.0
ROPE_DIM = HEAD_DIM // 4
SGU_CHUNK = 128
SGU_GROUPS = 4
POOL_WINDOWS = (2, 4, 8, 16)
POOL_HALO = 16
D_FF = 2816
DN_ALPHA = (2 * DEPTH) ** 0.25
LN_EPS = 1e-5
LOG2_E = 1.4426950408889634
QK_SCALE = HEAD_DIM ** -0.5 * LOG2_E
MASK_BIAS = -1e30
SKIP_LOG2 = 160.0
NORM_SLACK = 1.02

FRONT_ROWS = MOBA_BLOCK
ATTN_TQ = 512
ATTN_TK = 256
ATTN_HEADS = 4
PROJ_ROWS = 1024
FFN_ROWS = 512
FFN_COLS = 256
FFN_HALO = 16
VMEM_LIMIT = 56 * 1024 * 1024

COL_A = 0
COL_B = 3 * GROUP_W
COL_FORGET = 6 * GROUP_W
REST_W = 3 * GROUP_W + 128


def _layer_norm_rows(r, g, b):
    mu = jnp.mean(r, axis=-1, keepdims=True)
    d = r - mu
    var = jnp.mean(d * d, axis=-1, keepdims=True)
    return d * lax.rsqrt(var + LN_EPS) * g + b


def _split_bf16(t):
    hi = t.astype(BF16)
    lo = (t - hi.astype(F32)).astype(BF16)
    return hi, lo


def _group_mean(t, avg):
    hi, lo = _split_bf16(t)
    return (jnp.dot(hi, avg, preferred_element_type=F32) + jnp.dot(lo, avg, preferred_element_type=F32))


def _front_kernel(x_ref, wqkv_ref, wrest_ref, rot_ref, bfg_ref, lng_ref, lnb_ref, sw_ref, sb_ref,
                  avg_ref, pw_ref, ps_ref,
                  qa_ref, ka_ref, va_ref, qb_ref, kb_ref, vb_ref, yc_ref, yd_ref, first_ref,
                  kbar_ref, carry_ref, halo_ref, hist_ref):
    rows = FRONT_ROWS
    i = pl.program_id(1)
    n_blk = kbar_ref.shape[1]

    @pl.when(i == 0)
    def _():
        kbar_ref[...] = jnp.zeros_like(kbar_ref)
        carry_ref[...] = jnp.zeros_like(carry_ref)
        halo_ref[...] = jnp.zeros_like(halo_ref)
        hist_ref[...] = jnp.zeros_like(hist_ref)

    xb = x_ref[0].astype(BF16)
    lane = lax.broadcasted_iota(jnp.int32, (rows, PAIR_W), 1)
    ones_slab = jnp.where(lax.broadcasted_iota(jnp.int32, (V_ROWS - HEAD_DIM, rows), 0) == 0, 1.0, 0.0)

    def head_rows(pair_t, e):
        return pair_t[e * HEAD_DIM:(e + 1) * HEAD_DIM]

    def head_lanes_first(pair, e):
        return pair if e == 0 else pltpu.roll(pair, HEAD_DIM, 1)

    za = jnp.dot(xb, wqkv_ref[:, COL_A:COL_A + 3 * GROUP_W], preferred_element_type=F32)
    cos_t, sin_a, sin_b = rot_ref[0], rot_ref[1], rot_ref[2]

    def rotary(t):
        return (t * cos_t + pltpu.roll(t, PAIR_W - ROPE_DIM // 2, 1) * sin_a
                + pltpu.roll(t, ROPE_DIM // 2, 1) * sin_b)

    blk_row = lax.broadcasted_iota(jnp.int32, (n_blk, rows), 0)
    blk_lane = lax.broadcasted_iota(jnp.int32, (n_blk, PAIR_W), 1)
    for pr in range(N_HEADS // 2):
        q2 = rotary(za[:, pr * PAIR_W:(pr + 1) * PAIR_W])
        k2 = rotary(za[:, GROUP_W + pr * PAIR_W:GROUP_W + (pr + 1) * PAIR_W])
        q2_t = (q2 * QK_SCALE).T
        v2_t = za[:, 2 * GROUP_W + pr * PAIR_W:2 * GROUP_W + (pr + 1) * PAIR_W].T
        for e in range(2):
            h = 2 * pr + e
            kbar = jnp.where(blk_lane // HEAD_DIM == e, kbar_ref[pr], 0.0)
            gate_t = lax.dot_general(kbar, q2, (((1,), (1,)), ((), ())),
                                     precision=lax.Precision.HIGHEST, preferred_element_type=F32)
            g = jnp.where(blk_row < i, gate_t, -jnp.inf)
            chosen = jnp.zeros((n_blk, rows), F32)
            for _ in range(MOBA_TOPK):
                top = jnp.max(g, axis=0, keepdims=True)
                first = jnp.min(jnp.where(g == top, blk_row, n_blk), axis=0, keepdims=True)
                pick = blk_row == first
                chosen = jnp.where(pick, 1.0, chosen)
                g = jnp.where(pick, -jnp.inf, g)
            keep = jnp.where(blk_row < i, chosen, jnp.where(blk_row == i, 1.0, 0.0))
            bias_t = jnp.where(keep > 0.0, 0.0, MASK_BIAS)
            qa_ref[0, h] = jnp.concatenate(
                [head_rows(q2_t, e), bias_t, jnp.zeros((HEAD_PAD - HEAD_DIM - n_blk, rows), F32)],
                axis=0).astype(BF16)
            ka_ref[0, h] = jnp.where(lane < HEAD_DIM, head_lanes_first(k2, e),
                                     jnp.where(lane == HEAD_DIM + i, 1.0, 0.0)).astype(BF16)
            va_ref[0, h, 0] = jnp.concatenate([head_rows(v2_t, e), ones_slab], axis=0).astype(BF16)
        kbar_ref[pr, pl.ds(i, 1), :] = jnp.sum(k2, axis=0, keepdims=True) * (1.0 / MOBA_BLOCK)

    zr = jnp.dot(xb, wrest_ref[...], preferred_element_type=F32)
    cu = zr[:, 0:GROUP_W]
    cv = zr[:, GROUP_W:2 * GROUP_W]
    dp = zr[:, 2 * GROUP_W:3 * GROUP_W]
    fl = zr[:, 3 * GROUP_W:3 * GROUP_W + 128] + bfg_ref[...]
    log_f = jnp.minimum(fl, 0.0) - jnp.log1p(jnp.exp(-jnp.abs(fl)))
    log_f = jnp.where(lane < N_HEADS, log_f, 0.0)
    r_i = lax.broadcasted_iota(jnp.int32, (rows, rows), 0)
    c_i = lax.broadcasted_iota(jnp.int32, (rows, rows), 1)
    tri = jnp.where(c_i <= r_i, 1.0, 0.0).astype(BF16)
    f_hi = log_f.astype(BF16)
    f_res = log_f - f_hi.astype(F32)
    f_mid = f_res.astype(BF16)
    f_lo = (f_res - f_mid.astype(F32)).astype(BF16)
    cum = (jnp.dot(tri, f_hi, preferred_element_type=F32) + jnp.dot(tri, f_mid, preferred_element_type=F32)
           + jnp.dot(tri, f_lo, preferred_element_type=F32) + carry_ref[0:1, :])
    carry_ref[0:1, :] = cum[rows - 1:rows, :]

    cum = cum * LOG2_E
    zb = jnp.dot(xb, wqkv_ref[:, COL_B:COL_B + 3 * GROUP_W], preferred_element_type=F32)
    cum_t = cum.T
    sub8 = lax.broadcasted_iota(jnp.int32, (8, rows), 0)
    tile_lane = lax.broadcasted_iota(jnp.int32, (1, 128), 1)
    first_needed = jnp.zeros((1, 128), jnp.int32)
    for pr in range(N_HEADS // 2):
        q2_t = (zb[:, pr * PAIR_W:(pr + 1) * PAIR_W] * QK_SCALE).T
        k2 = zb[:, GROUP_W + pr * PAIR_W:GROUP_W + (pr + 1) * PAIR_W]
        v2_t = zb[:, 2 * GROUP_W + pr * PAIR_W:2 * GROUP_W + (pr + 1) * PAIR_W].T
        for e in range(2):
            h = 2 * pr + e
            q_norm = jnp.sqrt(jnp.max(jnp.sum(jnp.square(head_rows(q2_t, e)), axis=0, keepdims=True),
                                      axis=1, keepdims=True)) * NORM_SLACK
            k_sq = jnp.where(lane // HEAD_DIM == e, jnp.square(k2), 0.0)
            k_norm = jnp.sqrt(jnp.max(jnp.sum(k_sq, axis=1, keepdims=True), axis=0, keepdims=True))
            hist_ref[h:h + 1, :] = jnp.where(tile_lane == i, k_norm, hist_ref[h:h + 1, :])
            hist_ref[N_HEADS + h:N_HEADS + h + 1, :] = jnp.where(
                tile_lane == i, cum[rows - 1:rows, h:h + 1], hist_ref[N_HEADS + h:N_HEADS + h + 1, :])
            upper = q_norm * hist_ref[h:h + 1, :] + cum[0:1, h:h + 1] - hist_ref[N_HEADS + h:N_HEADS + h + 1, :]
            skippable = (upper < -(q_norm * k_norm) - SKIP_LOG2) & (tile_lane < i)
            first_h = jnp.min(jnp.where(skippable, 128, tile_lane), axis=1, keepdims=True)
            first_needed = jnp.where(tile_lane == h, first_h, first_needed)
            c = jnp.broadcast_to(cum[:, h:h + 1], (rows, PAIR_W))
            hi = c.astype(BF16).astype(F32)
            mid = (c - hi).astype(BF16).astype(F32)
            lo = c - hi - mid
            k_aug = jnp.where(lane < HEAD_DIM, head_lanes_first(k2, e),
                    jnp.where(lane < HEAD_DIM + 3, 1.0,
                    jnp.where(lane == HEAD_DIM + 3, -hi,
                    jnp.where(lane == HEAD_DIM + 4, -mid,
                    jnp.where(lane == HEAD_DIM + 5, -lo, 0.0)))))
            c_t = jnp.broadcast_to(cum_t[h:h + 1, :], (8, rows))
            hi_t = c_t.astype(BF16).astype(F32)
            mid_t = (c_t - hi_t).astype(BF16).astype(F32)
            lo_t = c_t - hi_t - mid_t
            q_bias = jnp.where(sub8 == 0, hi_t, jnp.where(sub8 == 1, mid_t, jnp.where(sub8 == 2, lo_t,
                     jnp.where(sub8 < 6, 1.0, 0.0))))
            qb_ref[0, h] = jnp.concatenate(
                [head_rows(q2_t, e), q_bias, jnp.zeros((HEAD_PAD - HEAD_DIM - 8, rows), F32)],
                axis=0).astype(BF16)
            kb_ref[0, h] = k_aug.astype(BF16)
            vb_ref[0, h, 0] = jnp.concatenate([head_rows(v2_t, e), ones_slab], axis=0).astype(BF16)
    first_ref[0, 0] = jnp.broadcast_to(first_needed, (8, 128))

    inv_sqrt2 = np.float32(1.0 / np.sqrt(2.0))
    u = 0.5 * cu * (1.0 + lax.erf(cu * inv_sqrt2))
    vg = 0.5 * cv * (1.0 + lax.erf(cv * inv_sqrt2))
    avg = avg_ref[...]
    mu = _group_mean(vg, avg)
    dv = vg - mu
    var = _group_mean(dv * dv, avg)
    vn = dv * lax.rsqrt(var + LN_EPS) * lng_ref[...] + lnb_ref[...]
    lane_w = lax.broadcasted_iota(jnp.int32, (SGU_CHUNK, GROUP_W), 1)
    t_i = lax.broadcasted_iota(jnp.int32, (SGU_CHUNK, SGU_CHUNK), 0)
    s_i = lax.broadcasted_iota(jnp.int32, (SGU_CHUNK, SGU_CHUNK), 1)
    w_tril = [jnp.where(s_i <= t_i, sw_ref[g], 0.0).astype(BF16) for g in range(SGU_GROUPS)]
    for c in range(rows // SGU_CHUNK):
        vn_c = vn[c * SGU_CHUNK:(c + 1) * SGU_CHUNK, :].astype(BF16)
        mixed = sb_ref[...]
        for g in range(SGU_GROUPS):
            mg = jnp.dot(w_tril[g], vn_c, preferred_element_type=F32)
            mixed = mixed + jnp.where(lane_w // HEAD_DIM == g, mg, 0.0)
        yc_ref[0, c * SGU_CHUNK:(c + 1) * SGU_CHUNK, :] = (
            u[c * SGU_CHUNK:(c + 1) * SGU_CHUNK, :] * mixed).astype(BF16)

    ext = jnp.concatenate([halo_ref[...], dp], axis=0)
    s2 = ext + pltpu.roll(ext, 1, 0)
    s4 = s2 + pltpu.roll(s2, 2, 0)
    s8 = s4 + pltpu.roll(s4, 4, 0)
    s16 = s8 + pltpu.roll(s8, 8, 0)
    lane_g = lax.broadcasted_iota(jnp.int32, (rows, GROUP_W), 1) // (GROUP_W // len(POOL_WINDOWS))
    t_glob = lax.broadcasted_iota(jnp.int32, (rows, GROUP_W), 0) + i * rows
    win = jnp.where(lane_g == 0, POOL_WINDOWS[0], jnp.where(lane_g == 1, POOL_WINDOWS[1],
          jnp.where(lane_g == 2, POOL_WINDOWS[2], POOL_WINDOWS[3])))
    wsum = jnp.where(lane_g == 0, s2[POOL_HALO:], jnp.where(lane_g == 1, s4[POOL_HALO:],
           jnp.where(lane_g == 2, s8[POOL_HALO:], s16[POOL_HALO:])))
    count = jnp.minimum(t_glob + 1, win).astype(F32)
    pooled = wsum / count - dp
    yd = jnp.dot(pooled.astype(BF16), pw_ref[...], preferred_element_type=F32) * ps_ref[...]
    yd_ref[0] = yd.astype(BF16)
    halo_ref[...] = dp[rows - POOL_HALO:, :]


def _mixer_front(x, layer, w_in, w_rest, rot, bfg, lng, lnb, sgu_w, sgu_bias, avg, pool_w, pool_s):
    bsz, seq, _ = x.shape
    n_blk = seq // MOBA_BLOCK
    assert seq % ATTN_TK == 0 and ATTN_TK % FRONT_ROWS == 0
    assert n_blk <= HEAD_PAD - HEAD_DIM and n_blk % 8 == 0 and seq // FRONT_ROWS <= 128
    rows = FRONT_ROWS
    const = lambda *shape: pl.BlockSpec(shape, lambda b, i: (0,) * len(shape))
    of_layer = lambda *shape: pl.BlockSpec((None,) + shape, lambda b, i: (layer,) + (0,) * len(shape))
    per_tk = ATTN_TK // rows
    q_spec = pl.BlockSpec((1, N_HEADS, HEAD_PAD, rows), lambda b, i: (b, 0, 0, i))
    k_spec = pl.BlockSpec((1, N_HEADS, rows, HEAD_PAD), lambda b, i: (b, 0, i, 0))
    v_spec = pl.BlockSpec((1, N_HEADS, 1, V_ROWS, rows), lambda b, i: (b, 0, i // per_tk, 0, i % per_tk))
    flat_spec = pl.BlockSpec((1, rows, GROUP_W), lambda b, i: (b, i, 0))
    q_shape = jax.ShapeDtypeStruct((bsz, N_HEADS, HEAD_PAD, seq), BF16)
    k_shape = jax.ShapeDtypeStruct((bsz, N_HEADS, seq, HEAD_PAD), BF16)
    v_shape = jax.ShapeDtypeStruct((bsz, N_HEADS, seq // ATTN_TK, V_ROWS, ATTN_TK), BF16)
    flat_shape = jax.ShapeDtypeStruct((bsz, seq, GROUP_W), BF16)
    return pl.pallas_call(
        _front_kernel,
        grid=(bsz, seq // rows),
        in_specs=[
            pl.BlockSpec((1, rows, D_MODEL), lambda b, i: (b, i, 0)),
            of_layer(D_MODEL, COL_FORGET),
            of_layer(D_MODEL, REST_W),
            pl.BlockSpec((3, rows, PAIR_W), lambda b, i: (0, i, 0)),
            of_layer(1, 128), of_layer(1, GROUP_W), of_layer(1, GROUP_W),
            of_layer(SGU_GROUPS, SGU_CHUNK, SGU_CHUNK), of_layer(SGU_CHUNK, GROUP_W),
            const(GROUP_W, GROUP_W), of_layer(GROUP_W, GROUP_W), of_layer(1, GROUP_W),
        ],
        out_specs=[q_spec, k_spec, v_spec] * 2 + [flat_spec] * 2 + [
            pl.BlockSpec((1, 1, 8, 128), lambda b, i: (b, i, 0, 0))],
        out_shape=[q_shape, k_shape, v_shape] * 2 + [flat_shape] * 2 + [
            jax.ShapeDtypeStruct((bsz, seq // rows, 8, 128), jnp.int32)],
        scratch_shapes=[
            pltpu.VMEM((N_HEADS // 2, n_blk, PAIR_W), F32),
            pltpu.VMEM((8, 128), F32),
            pltpu.VMEM((POOL_HALO, GROUP_W), F32),
            pltpu.VMEM((2 * N_HEADS, 128), F32),
        ],
        compiler_params=pltpu.CompilerParams(
            dimension_semantics=("arbitrary", "arbitrary"), vmem_limit_bytes=VMEM_LIMIT),
        name="mixer_front",
    )(x, w_in, w_rest, rot, bfg, lng, lnb, sgu_w, sgu_bias, avg, pool_w, pool_s)


def _attn_kernel(first_ref, qt_ref, k_ref, vt_ref, o_ref, acc_ref, sa_ref, sb_ref, mta_ref, mtb_ref):
    tq, tk = ATTN_TQ, ATTN_TK
    qi = pl.program_id(2)
    acc_ref[...] = jnp.zeros_like(acc_ref)
    n_sub = pl.num_programs(2) * (tq // tk)
    n_heads = pl.num_programs(1) * ATTN_HEADS
    first = 2 * qi
    for sub in range(tq // tk):
        for h in range(ATTN_HEADS):
            first = jnp.minimum(first, first_ref[((pl.program_id(0) * n_sub + 2 * qi + sub) * n_heads
                                                  + pl.program_id(1) * ATTN_HEADS + h)])
    first_pair = first // 2

    def produce(j, s_ref, mt_ref):
        off = pl.multiple_of(j * tk, tk)
        for h in range(ATTN_HEADS):
            s = jnp.dot(k_ref[0, h, pl.ds(off, tk), :], qt_ref[0, h], preferred_element_type=F32)
            s_ref[h] = s
            mt_ref[h, 0:1, :] = jnp.max(s, axis=0, keepdims=True)

    def consume(j, s_ref, mt_ref, ms, masked):
        new_ms = []
        for h in range(ATTN_HEADS):
            if masked:
                key_pos = lax.broadcasted_iota(jnp.int32, (tk, tq), 0) + j * tk
                qry_pos = lax.broadcasted_iota(jnp.int32, (tk, tq), 1) + qi * tq
                s = jnp.where(key_pos <= qry_pos, s_ref[h], -jnp.inf)
                m_tile = jnp.max(s, axis=0, keepdims=True)
            else:
                s = s_ref[h]
                m_tile = mt_ref[h, 0:1, :]
            m_new = jnp.maximum(ms[h], m_tile)
            alpha = jnp.exp2(ms[h] - m_new)
            p = jnp.exp2(s - m_new).astype(BF16)
            acc_ref[h] = alpha * acc_ref[h] + jnp.dot(vt_ref[0, h, j], p, preferred_element_type=F32)
            new_ms.append(m_new)
        return tuple(new_ms)

    def pair(jj, ms):
        produce(2 * jj + 1, sb_ref, mtb_ref)
        ms = consume(2 * jj, sa_ref, mta_ref, ms, False)
        produce(2 * jj + 2, sa_ref, mta_ref)
        return consume(2 * jj + 1, sb_ref, mtb_ref, ms, False)

    produce(2 * first_pair, sa_ref, mta_ref)
    m0 = jnp.full((1, tq), -jnp.inf, F32)
    ms = lax.fori_loop(first_pair, qi, pair, (m0,) * ATTN_HEADS)
    produce(2 * qi + 1, sb_ref, mtb_ref)
    ms = consume(2 * qi, sa_ref, mta_ref, ms, True)
    consume(2 * qi + 1, sb_ref, mtb_ref, ms, True)

    outs = [(acc_ref[h] / acc_ref[h, HEAD_DIM:HEAD_DIM + 1, :])[0:HEAD_DIM] for h in range(ATTN_HEADS)]
    o_ref[0] = jnp.concatenate(outs, axis=0).T.astype(BF16)


def _causal_attn(qt, k, vt, first_tile):
    bsz, n_heads, seq, _ = k.shape
    assert ATTN_TQ == 2 * ATTN_TK and ATTN_TK == FRONT_ROWS and seq % ATTN_TQ == 0 and n_heads % ATTN_HEADS == 0
    grid_spec = pltpu.PrefetchScalarGridSpec(
        num_scalar_prefetch=1,
        grid=(bsz, n_heads // ATTN_HEADS, seq // ATTN_TQ),
        in_specs=[
            pl.BlockSpec((1, ATTN_HEADS, HEAD_PAD, ATTN_TQ), lambda b, hp, i, first: (b, hp, 0, i)),
            pl.BlockSpec((1, ATTN_HEADS, seq, HEAD_PAD), lambda b, hp, i, first: (b, hp, 0, 0)),
            pl.BlockSpec((1, ATTN_HEADS, seq // ATTN_TK, V_ROWS, ATTN_TK), lambda b, hp, i, first: (b, hp, 0, 0, 0)),
        ],
        out_specs=pl.BlockSpec((1, ATTN_TQ, ATTN_HEADS * HEAD_DIM), lambda b, hp, i, first: (b, i, hp)),
        scratch_shapes=[pltpu.VMEM((ATTN_HEADS, V_ROWS, ATTN_TQ), F32),
                        pltpu.VMEM((ATTN_HEADS, ATTN_TK, ATTN_TQ), F32),
                        pltpu.VMEM((ATTN_HEADS, ATTN_TK, ATTN_TQ), F32),
                        pltpu.VMEM((ATTN_HEADS, 8, ATTN_TQ), F32),
                        pltpu.VMEM((ATTN_HEADS, 8, ATTN_TQ), F32)],
    )
    return pl.pallas_call(
        _attn_kernel,
        grid_spec=grid_spec,
        out_shape=jax.ShapeDtypeStruct((bsz, seq, n_heads * HEAD_DIM), BF16),
        compiler_params=pltpu.CompilerParams(
            dimension_semantics=("parallel", "parallel", "arbitrary"), vmem_limit_bytes=VMEM_LIMIT),
        name="causal_attn",
    )(first_tile, qt, k, vt)


def _out_proj_kernel(ya_ref, yb_ref, yc_ref, yd_ref, x_ref, w_ref, g_ref, b_ref, o_ref):
    half = PROJ_ROWS // 2
    spans = [slice(i * half, (i + 1) * half) for i in range(2)]
    ys = []
    for sp in spans:
        y = jnp.concatenate([ya_ref[sp, :], yb_ref[sp, :], yc_ref[sp, :], yd_ref[sp, :]], axis=1)
        ys.append(jnp.dot(y, w_ref[...], preferred_element_type=F32))
    for sp, y in zip(spans, ys):
        o_ref[sp, :] = _layer_norm_rows(DN_ALPHA * x_ref[sp, :] + y, g_ref[...], b_ref[...])


def _out_proj_ln(ya, yb, yc, yd, x, layer, w_o, g, b):
    m = x.shape[0]
    rows = PROJ_ROWS
    assert m % rows == 0
    part = pl.BlockSpec((rows, GROUP_W), lambda i: (i, 0))
    vec = pl.BlockSpec((None, 1, D_MODEL), lambda i: (layer, 0, 0))
    return pl.pallas_call(
        _out_proj_kernel,
        grid=(m // rows,),
        in_specs=[part, part, part, part,
                  pl.BlockSpec((rows, D_MODEL), lambda i: (i, 0)),
                  pl.BlockSpec((None, D_MODEL, D_MODEL), lambda i: (layer, 0, 0)), vec, vec],
        out_specs=pl.BlockSpec((rows, D_MODEL), lambda i: (i, 0)),
        out_shape=jax.ShapeDtypeStruct((m, D_MODEL), F32),
        compiler_params=pltpu.CompilerParams(dimension_semantics=("parallel",), vmem_limit_bytes=VMEM_LIMIT),
        name="out_proj_ln",
    )(ya, yb, yc, yd, x, w_o, g, b)


def _ffn_kernel(tiles_per_seq, x_ref, xp_ref, wu_ref, cw_ref, cb_ref, wd_ref, g_ref, b_ref, o_ref,
                xb_ref, ha_ref, hb_ref, acc_ref):
    m = pl.program_id(0)
    cols = FFN_COLS
    n_col = D_FF // cols
    prev = jnp.where(m % tiles_per_seq == 0, 0.0, xp_ref[...])
    xb_ref[0:FFN_HALO, :] = prev.astype(BF16)
    xb_ref[FFN_HALO:, :] = x_ref[...].astype(BF16)

    def both(n, ref):
        return jnp.concatenate([ref[:, n * cols:(n + 1) * cols], ref[:, D_FF + n * cols:D_FF + (n + 1) * cols]],
                               axis=1)

    def up(n, h_ref):
        xb = xb_ref[...]
        h_ref[:, 0:cols] = jnp.dot(xb, wu_ref[:, n * cols:(n + 1) * cols], preferred_element_type=F32)
        h_ref[:, cols:2 * cols] = jnp.dot(xb, wu_ref[:, D_FF + n * cols:D_FF + (n + 1) * cols],
                                          preferred_element_type=F32)

    def gated(n, h_ref):
        h = h_ref[...]
        cw = both(n, cw_ref)
        c = (pltpu.roll(h, 2, 0) * cw[0:1, :] + pltpu.roll(h, 1, 0) * cw[1:2, :] + h * cw[2:3, :]
             + both(n, cb_ref))[FFN_HALO:, :]
        gate, val = c[:, 0:cols], c[:, cols:2 * cols]
        return (gate * jax.nn.sigmoid(gate) * val).astype(BF16)

    def down(n, act):
        return jnp.dot(act, wd_ref[n * cols:(n + 1) * cols, :], preferred_element_type=F32)

    bufs = (ha_ref, hb_ref)
    up(0, bufs[0])
    parts = []
    for n in range(n_col):
        if n + 1 < n_col:
            up(n + 1, bufs[(n + 1) % 2])
        parts.append(down(n, gated(n, bufs[n % 2])))
        if len(parts) == 2:
            if n == 1:
                acc_ref[...] = parts[0] + parts[1]
            else:
                acc_ref[...] += parts[0] + parts[1]
            parts = []
    y = acc_ref[...] + parts[0]
    o_ref[...] = _layer_norm_rows(DN_ALPHA * x_ref[...] + y, g_ref[...], b_ref[...])


def _conv_ffn_ln(x, seq, layer, w_up, conv_w, conv_b, w_down, g, b):
    m = x.shape[0]
    rows = FFN_ROWS
    n_col = D_FF // FFN_COLS
    assert m % rows == 0 and seq % rows == 0 and rows % FFN_HALO == 0 and n_col % 2 == 1 and n_col >= 3
    halo_blocks = rows // FFN_HALO
    resident = lambda *shape: pl.BlockSpec((None,) + shape, lambda i: (layer,) + (0,) * len(shape),
                                           pipeline_mode=pl.Buffered(1))
    return pl.pallas_call(
        functools.partial(_ffn_kernel, seq // rows),
        grid=(m // rows,),
        in_specs=[
            pl.BlockSpec((rows, D_MODEL), lambda i: (i, 0)),
            pl.BlockSpec((FFN_HALO, D_MODEL), lambda i: (jnp.maximum(i * halo_blocks - 1, 0), 0)),
            resident(D_MODEL, 2 * D_FF),
            resident(3, 2 * D_FF),
            resident(1, 2 * D_FF),
            resident(D_FF, D_MODEL),
            resident(1, D_MODEL), resident(1, D_MODEL),
        ],
        out_specs=pl.BlockSpec((rows, D_MODEL), lambda i: (i, 0)),
        out_shape=jax.ShapeDtypeStruct((m, D_MODEL), F32),
        scratch_shapes=[pltpu.VMEM((FFN_HALO + rows, D_MODEL), BF16),
                        pltpu.VMEM((FFN_HALO + rows, 2 * FFN_COLS), F32),
                        pltpu.VMEM((FFN_HALO + rows, 2 * FFN_COLS), F32),
                        pltpu.VMEM((rows, D_MODEL), F32)],
        compiler_params=pltpu.CompilerParams(dimension_semantics=("parallel",), vmem_limit_bytes=VMEM_LIMIT),
        name="conv_ffn_ln",
    )(x, x, w_up, conv_w, conv_b, w_down, g, b)


def _rotary_lane_tables(seq):
    pos = jnp.arange(seq, dtype=F32)
    inv_freq = ROPE_THETA ** (-jnp.arange(0, ROPE_DIM, 2, dtype=F32) / ROPE_DIM)
    half = ROPE_DIM // 2
    d = np.arange(PAIR_W) % HEAD_DIM
    freq_lane = jnp.where(d < ROPE_DIM, inv_freq[d % half], 0.0)
    ang = pos[:, None] * freq_lane[None, :]
    cos, sin = jnp.cos(ang), jnp.sin(ang)
    sin_a = jnp.where(d < half, -sin, 0.0)
    sin_b = jnp.where((d >= half) & (d < ROPE_DIM), sin, 0.0)
    return jnp.stack([cos, sin_a, sin_b])


def kernel(x, w_in, b_forget, sgu_ln_g, sgu_ln_b, sgu_w, sgu_b, pool_w, pool_scale, w_o, ln1_g, ln1_b,
           w_up, conv_w, conv_b, w_down, ln2_g, ln2_b):
    bsz, seq, _ = x.shape
    depth = w_in.shape[0]
    row = lambda t: t.reshape(depth, 1, t.shape[-1])
    rot = _rotary_lane_tables(seq)
    group = jnp.arange(GROUP_W) // HEAD_DIM
    avg = jnp.where(group[:, None] == group[None, :], 1.0 / HEAD_DIM, 0.0).astype(BF16)
    f0 = COL_FORGET
    w_qkv = w_in[:, :, :f0].astype(BF16)
    w_rest = jnp.concatenate([w_in[:, :, f0 + N_HEADS:], w_in[:, :, f0:f0 + N_HEADS],
                              jnp.zeros((depth, D_MODEL, 128 - N_HEADS), F32)], axis=-1).astype(BF16)
    bfg = jnp.pad(b_forget, ((0, 0), (0, 128 - N_HEADS))).reshape(depth, 1, 128)
    sgu_bias = jnp.repeat(sgu_b.transpose(0, 2, 1), HEAD_DIM, axis=2)
    n_pool = len(POOL_WINDOWS)
    same_group = jnp.eye(n_pool, dtype=bool)[None, :, None, :, None]
    pool_bd = jnp.where(same_group, pool_w[:, :, :, None, :], 0.0).reshape(
        depth, GROUP_W, GROUP_W).astype(BF16)
    w_o_b, w_up_b, w_down_b = w_o.astype(BF16), w_up.astype(BF16), w_down.astype(BF16)

    xf = x.reshape(bsz * seq, D_MODEL)
    for l in range(depth):
        qa, ka, va, qb, kb, vb, yc, yd, first = _mixer_front(
            xf.reshape(bsz, seq, D_MODEL), l, w_qkv, w_rest, rot, bfg, row(sgu_ln_g), row(sgu_ln_b), sgu_w,
            sgu_bias, avg, pool_bd, row(pool_scale))
        first_b = first[:, :, 0, :N_HEADS].reshape(-1)
        ya = _causal_attn(qa, ka, va, jnp.zeros_like(first_b)).reshape(bsz * seq, GROUP_W)
        yb = _causal_attn(qb, kb, vb, first_b).reshape(bsz * seq, GROUP_W)
        xf = _out_proj_ln(ya, yb, yc.reshape(bsz * seq, GROUP_W), yd.reshape(bsz * seq, GROUP_W), xf, l,
                          w_o_b, row(ln1_g), row(ln1_b))
        xf = _conv_ffn_ln(xf, seq, l, w_up_b, conv_w, row(conv_b), w_down_b, row(ln2_g), row(ln2_b))
    return xf.reshape(bsz, seq, D_MODEL)
```

```python
import functools

import numpy as np
import jax
import jax.numpy as jnp
from jax import lax
from jax.experimental import pallas as pl
from jax.experimental.pallas import tpu as pltpu

F32 = jnp.float32
BF16 = jnp.bfloat16

D_MODEL = 1024
DEPTH = 2
HEAD_DIM = 64
N_HEADS = 4
GROUP_W = 256
HEAD_PAD = 128
PAIR_W = 2 * HEAD_DIM
V_ROWS = 80
MOBA_BLOCK = 256
MOBA_TOPK = 3
ROPE_THETA = 500000.0
ROPE_DIM = HEAD_DIM // 4
SGU_CHUNK = 128
SGU_GROUPS = 4
POOL_WINDOWS = (2, 4, 8, 16)
POOL_HALO = 16
D_FF = 2816
DN_ALPHA = (2 * DEPTH) ** 0.25
LN_EPS = 1e-5
LOG2_E = 1.4426950408889634
QK_SCALE = HEAD_DIM ** -0.5 * LOG2_E
MASK_BIAS = -1e30
SKIP_LOG2 = 160.0
NORM_SLACK = 1.02

FRONT_ROWS = MOBA_BLOCK
ATTN_TQ = 512
ATTN_TK = 256
ATTN_HEADS = 4
PROJ_ROWS = 1024
FFN_ROWS = 512
FFN_COLS = 256
FFN_HALO = 16
VMEM_LIMIT = 56 * 1024 * 1024

COL_A = 0
COL_B = 3 * GROUP_W
COL_FORGET = 6 * GROUP_W
REST_W = 3 * GROUP_W + 128


def _layer_norm_rows(r, g, b):
    mu = jnp.mean(r, axis=-1, keepdims=True)
    d = r - mu
    var = jnp.mean(d * d, axis=-1, keepdims=True)
    return d * lax.rsqrt(var + LN_EPS) * g + b


def _split_bf16(t):
    hi = t.astype(BF16)
    lo = (t - hi.astype(F32)).astype(BF16)
    return hi, lo


def _group_mean(t, avg):
    hi, lo = _split_bf16(t)
    return (jnp.dot(hi, avg, preferred_element_type=F32) + jnp.dot(lo, avg, preferred_element_type=F32))


def _front_kernel(x_ref, wqkv_ref, wrest_ref, rot_ref, bfg_ref, lng_ref, lnb_ref, sw_ref, sb_ref,
                  avg_ref, pw_ref, ps_ref,
                  qa_ref, ka_ref, va_ref, qb_ref, kb_ref, vb_ref, yc_ref, yd_ref, first_ref,
                  kbar_ref, carry_ref, halo_ref, hist_ref):
    rows = FRONT_ROWS
    i = pl.program_id(1)
    n_blk = kbar_ref.shape[1]

    @pl.when(i == 0)
    def _():
        kbar_ref[...] = jnp.zeros_like(kbar_ref)
        carry_ref[...] = jnp.zeros_like(carry_ref)
        halo_ref[...] = jnp.zeros_like(halo_ref)
        hist_ref[...] = jnp.zeros_like(hist_ref)

    xb = x_ref[0].astype(BF16)
    lane = lax.broadcasted_iota(jnp.int32, (rows, PAIR_W), 1)
    ones_slab = jnp.where(lax.broadcasted_iota(jnp.int32, (V_ROWS - HEAD_DIM, rows), 0) == 0, 1.0, 0.0)

    def head_rows(pair_t, e):
        return pair_t[e * HEAD_DIM:(e + 1) * HEAD_DIM]

    def head_lanes_first(pair, e):
        return pair if e == 0 else pltpu.roll(pair, HEAD_DIM, 1)

    cv = jnp.dot(xb, wrest_ref[:, GROUP_W:2 * GROUP_W], preferred_element_type=F32)
    cu = jnp.dot(xb, wrest_ref[:, 0:GROUP_W], preferred_element_type=F32)
    dp_fl = jnp.dot(xb, wrest_ref[:, 2 * GROUP_W:REST_W], preferred_element_type=F32)
    za = jnp.dot(xb, wqkv_ref[:, COL_A:COL_A + 3 * GROUP_W], preferred_element_type=F32)
    cos_t, sin_a, sin_b = rot_ref[0], rot_ref[1], rot_ref[2]

    def rotary(t):
        return (t * cos_t + pltpu.roll(t, PAIR_W - ROPE_DIM // 2, 1) * sin_a
                + pltpu.roll(t, ROPE_DIM // 2, 1) * sin_b)

    dp = dp_fl[:, 0:GROUP_W]

    inv_sqrt2 = np.float32(1.0 / np.sqrt(2.0))
    u = 0.5 * cu * (1.0 + lax.erf(cu * inv_sqrt2))
    vg = 0.5 * cv * (1.0 + lax.erf(cv * inv_sqrt2))
    avg = avg_ref[...]
    mu = _group_mean(vg, avg)

    sub8 = lax.broadcasted_iota(jnp.int32, (8, rows), 0)
    fl = dp_fl[:, GROUP_W:GROUP_W + 128].T[0:8, :] + jnp.broadcast_to(bfg_ref[:, 0:1], (8, rows))
    log_f = jnp.minimum(fl, 0.0) - jnp.log1p(jnp.exp(-jnp.abs(fl)))
    log_f = jnp.where(sub8 < N_HEADS, log_f, 0.0)
    r_i = lax.broadcasted_iota(jnp.int32, (rows, rows), 0)
    c_i = lax.broadcasted_iota(jnp.int32, (rows, rows), 1)
    tri = jnp.where(r_i <= c_i, 1.0, 0.0).astype(BF16)
    f_hi = log_f.astype(BF16)
    f_res = log_f - f_hi.astype(F32)
    f_mid = f_res.astype(BF16)
    f_lo = (f_res - f_mid.astype(F32)).astype(BF16)
    cum_t = (jnp.dot(f_hi, tri, preferred_element_type=F32) + jnp.dot(f_mid, tri, preferred_element_type=F32)
             + jnp.dot(f_lo, tri, preferred_element_type=F32) + jnp.broadcast_to(carry_ref[:, 0:1], (8, rows)))
    carry_ref[...] = jnp.broadcast_to(cum_t[:, rows - 1:rows], (8, 128))

    zb = jnp.dot(xb, wqkv_ref[:, COL_B:COL_B + 3 * GROUP_W], preferred_element_type=F32)

    dv = vg - mu
    var = _group_mean(dv * dv, avg)
    vn = dv * lax.rsqrt(var + LN_EPS) * lng_ref[...] + lnb_ref[...]
    lane_w = lax.broadcasted_iota(jnp.int32, (SGU_CHUNK, GROUP_W), 1)
    t_i = lax.broadcasted_iota(jnp.int32, (SGU_CHUNK, SGU_CHUNK), 0)
    s_i = lax.broadcasted_iota(jnp.int32, (SGU_CHUNK, SGU_CHUNK), 1)
    w_tril = [jnp.where(s_i <= t_i, sw_ref[g], 0.0).astype(BF16) for g in range(SGU_GROUPS)]
    for c in range(rows // SGU_CHUNK):
        vn_c = vn[c * SGU_CHUNK:(c + 1) * SGU_CHUNK, :].astype(BF16)
        mixed = sb_ref[...]
        for g in range(SGU_GROUPS):
            mg = jnp.dot(w_tril[g], vn_c, preferred_element_type=F32)
            mixed = mixed + jnp.where(lane_w // HEAD_DIM == g, mg, 0.0)
        yc_ref[0, c * SGU_CHUNK:(c + 1) * SGU_CHUNK, :] = (
            u[c * SGU_CHUNK:(c + 1) * SGU_CHUNK, :] * mixed).astype(BF16)

    ext = jnp.concatenate([halo_ref[...], dp], axis=0)
    s2 = ext + pltpu.roll(ext, 1, 0)
    s4 = s2 + pltpu.roll(s2, 2, 0)
    s8 = s4 + pltpu.roll(s4, 4, 0)
    s16 = s8 + pltpu.roll(s8, 8, 0)
    lane_g = lax.broadcasted_iota(jnp.int32, (rows, GROUP_W), 1) // (GROUP_W // len(POOL_WINDOWS))
    t_glob = lax.broadcasted_iota(jnp.int32, (rows, GROUP_W), 0) + i * rows
    win = jnp.where(lane_g == 0, POOL_WINDOWS[0], jnp.where(lane_g == 1, POOL_WINDOWS[1],
          jnp.where(lane_g == 2, POOL_WINDOWS[2], POOL_WINDOWS[3])))
    wsum = jnp.where(lane_g == 0, s2[POOL_HALO:], jnp.where(lane_g == 1, s4[POOL_HALO:],
           jnp.where(lane_g == 2, s8[POOL_HALO:], s16[POOL_HALO:])))
    count = jnp.minimum(t_glob + 1, win).astype(F32)
    pooled = wsum / count - dp
    yd = jnp.dot(pooled.astype(BF16), pw_ref[...], preferred_element_type=F32) * ps_ref[...]
    yd_ref[0] = yd.astype(BF16)
    halo_ref[...] = dp[rows - POOL_HALO:, :]

    blk_row = lax.broadcasted_iota(jnp.int32, (n_blk, rows), 0)
    blk_lane = lax.broadcasted_iota(jnp.int32, (n_blk, PAIR_W), 1)
    for pr in range(N_HEADS // 2):
        q2 = rotary(za[:, pr * PAIR_W:(pr + 1) * PAIR_W])
        k2 = rotary(za[:, GROUP_W + pr * PAIR_W:GROUP_W + (pr + 1) * PAIR_W])
        q2_t = (q2 * QK_SCALE).T
        v2_t = za[:, 2 * GROUP_W + pr * PAIR_W:2 * GROUP_W + (pr + 1) * PAIR_W].T
        for e in range(2):
            h = 2 * pr + e
            kbar = jnp.where(blk_lane // HEAD_DIM == e, kbar_ref[pr], 0.0)
            gate_t = lax.dot_general(kbar, q2, (((1,), (1,)), ((), ())),
                                     precision=lax.Precision.HIGHEST, preferred_element_type=F32)
            g = jnp.where(blk_row < i, gate_t, -jnp.inf)
            chosen = jnp.zeros((n_blk, rows), F32)
            for _ in range(MOBA_TOPK):
                top = jnp.max(g, axis=0, keepdims=True)
                first = jnp.min(jnp.where(g == top, blk_row, n_blk), axis=0, keepdims=True)
                pick = blk_row == first
                chosen = jnp.where(pick, 1.0, chosen)
                g = jnp.where(pick, -jnp.inf, g)
            keep = jnp.where(blk_row < i, chosen, jnp.where(blk_row == i, 1.0, 0.0))
            bias_t = jnp.where(keep > 0.0, 0.0, MASK_BIAS)
            qa_ref[0, h] = jnp.concatenate(
                [head_rows(q2_t, e), bias_t, jnp.zeros((HEAD_PAD - HEAD_DIM - n_blk, rows), F32)],
                axis=0).astype(BF16)
            ka_ref[0, h] = jnp.where(lane < HEAD_DIM, head_lanes_first(k2, e),
                                     jnp.where(lane == HEAD_DIM + i, 1.0, 0.0)).astype(BF16)
            va_ref[0, h, 0] = jnp.concatenate([head_rows(v2_t, e), ones_slab], axis=0).astype(BF16)
        kbar_ref[pr, pl.ds(i, 1), :] = jnp.sum(k2, axis=0, keepdims=True) * (1.0 / MOBA_BLOCK)

    cum_t = cum_t * LOG2_E
    cum = jnp.concatenate([cum_t, jnp.zeros((128 - 8, rows), F32)], axis=0).T
    tile_lane = lax.broadcasted_iota(jnp.int32, (1, 128), 1)
    first_needed = jnp.zeros((1, 128), jnp.int32)
    for pr in range(N_HEADS // 2):
        q2_t = (zb[:, pr * PAIR_W:(pr + 1) * PAIR_W] * QK_SCALE).T
        k2 = zb[:, GROUP_W + pr * PAIR_W:GROUP_W + (pr + 1) * PAIR_W]
        v2_t = zb[:, 2 * GROUP_W + pr * PAIR_W:2 * GROUP_W + (pr + 1) * PAIR_W].T
        for e in range(2):
            h = 2 * pr + e
            q_norm = jnp.sqrt(jnp.max(jnp.sum(jnp.square(head_rows(q2_t, e)), axis=0, keepdims=True),
                                      axis=1, keepdims=True)) * NORM_SLACK
            k_sq = jnp.where(lane // HEAD_DIM == e, jnp.square(k2), 0.0)
            k_norm = jnp.sqrt(jnp.max(jnp.sum(k_sq, axis=1, keepdims=True), axis=0, keepdims=True))
            hist_ref[h:h + 1, :] = jnp.where(tile_lane == i, k_norm, hist_ref[h:h + 1, :])
            hist_ref[N_HEADS + h:N_HEADS + h + 1, :] = jnp.where(
                tile_lane == i, cum[rows - 1:rows, h:h + 1], hist_ref[N_HEADS + h:N_HEADS + h + 1, :])
            upper = q_norm * hist_ref[h:h + 1, :] + cum[0:1, h:h + 1] - hist_ref[N_HEADS + h:N_HEADS + h + 1, :]
            skippable = (upper < -(q_norm * k_norm) - SKIP_LOG2) & (tile_lane < i)
            first_h = jnp.min(jnp.where(skippable, 128, tile_lane), axis=1, keepdims=True)
            first_needed = jnp.where(tile_lane == h, first_h, first_needed)
            c = jnp.broadcast_to(cum[:, h:h + 1], (rows, PAIR_W))
            hi = c.astype(BF16).astype(F32)
            mid = (c - hi).astype(BF16).astype(F32)
            lo = c - hi - mid
            k_aug = jnp.where(lane < HEAD_DIM, head_lanes_first(k2, e),
                    jnp.where(lane < HEAD_DIM + 3, 1.0,
                    jnp.where(lane == HEAD_DIM + 3, -hi,
                    jnp.where(lane == HEAD_DIM + 4, -mid,
                    jnp.where(lane == HEAD_DIM + 5, -lo, 0.0)))))
            c_t = jnp.broadcast_to(cum_t[h:h + 1, :], (8, rows))
            hi_t = c_t.astype(BF16).astype(F32)
            mid_t = (c_t - hi_t).astype(BF16).astype(F32)
            lo_t = c_t - hi_t - mid_t
            q_bias = jnp.where(sub8 == 0, hi_t, jnp.where(sub8 == 1, mid_t, jnp.where(sub8 == 2, lo_t,
                     jnp.where(sub8 < 6, 1.0, 0.0))))
            qb_ref[0, h] = jnp.concatenate(
                [head_rows(q2_t, e), q_bias, jnp.zeros((HEAD_PAD - HEAD_DIM - 8, rows), F32)],
                axis=0).astype(BF16)
            kb_ref[0, h] = k_aug.astype(BF16)
            vb_ref[0, h, 0] = jnp.concatenate([head_rows(v2_t, e), ones_slab], axis=0).astype(BF16)
    first_ref[0, 0] = jnp.broadcast_to(first_needed, (8, 128))


def _mixer_front(x, layer, w_in, w_rest, rot, bfg, lng, lnb, sgu_w, sgu_bias, avg, pool_w, pool_s):
    bsz, seq, _ = x.shape
    n_blk = seq // MOBA_BLOCK
    assert seq % ATTN_TK == 0 and ATTN_TK % FRONT_ROWS == 0
    assert n_blk <= HEAD_PAD - HEAD_DIM and n_blk % 8 == 0 and seq // FRONT_ROWS <= 128
    rows = FRONT_ROWS
    const = lambda *shape: pl.BlockSpec(shape, lambda b, i: (0,) * len(shape))
    of_layer = lambda *shape: pl.BlockSpec((None,) + shape, lambda b, i: (layer,) + (0,) * len(shape))
    per_tk = ATTN_TK // rows
    q_spec = pl.BlockSpec((1, N_HEADS, HEAD_PAD, rows), lambda b, i: (b, 0, 0, i))
    k_spec = pl.BlockSpec((1, N_HEADS, rows, HEAD_PAD), lambda b, i: (b, 0, i, 0))
    v_spec = pl.BlockSpec((1, N_HEADS, 1, V_ROWS, rows), lambda b, i: (b, 0, i // per_tk, 0, i % per_tk))
    flat_spec = pl.BlockSpec((1, rows, GROUP_W), lambda b, i: (b, i, 0))
    q_shape = jax.ShapeDtypeStruct((bsz, N_HEADS, HEAD_PAD, seq), BF16)
    k_shape = jax.ShapeDtypeStruct((bsz, N_HEADS, seq, HEAD_PAD), BF16)
    v_shape = jax.ShapeDtypeStruct((bsz, N_HEADS, seq // ATTN_TK, V_ROWS, ATTN_TK), BF16)
    flat_shape = jax.ShapeDtypeStruct((bsz, seq, GROUP_W), BF16)
    return pl.pallas_call(
        _front_kernel,
        grid=(bsz, seq // rows),
        in_specs=[
            pl.BlockSpec((1, rows, D_MODEL), lambda b, i: (b, i, 0)),
            of_layer(D_MODEL, COL_FORGET),
            of_layer(D_MODEL, REST_W),
            pl.BlockSpec((3, rows, PAIR_W), lambda b, i: (0, i, 0)),
            of_layer(8, 128), of_layer(1, GROUP_W), of_layer(1, GROUP_W),
            of_layer(SGU_GROUPS, SGU_CHUNK, SGU_CHUNK), of_layer(SGU_CHUNK, GROUP_W),
            const(GROUP_W, GROUP_W), of_layer(GROUP_W, GROUP_W), of_layer(1, GROUP_W),
        ],
        out_specs=[q_spec, k_spec, v_spec] * 2 + [flat_spec] * 2 + [
            pl.BlockSpec((1, 1, 8, 128), lambda b, i: (b, i, 0, 0))],
        out_shape=[q_shape, k_shape, v_shape] * 2 + [flat_shape] * 2 + [
            jax.ShapeDtypeStruct((bsz, seq // rows, 8, 128), jnp.int32)],
        scratch_shapes=[
            pltpu.VMEM((N_HEADS // 2, n_blk, PAIR_W), F32),
            pltpu.VMEM((8, 128), F32),
            pltpu.VMEM((POOL_HALO, GROUP_W), F32),
            pltpu.VMEM((2 * N_HEADS, 128), F32),
        ],
        compiler_params=pltpu.CompilerParams(
            dimension_semantics=("arbitrary", "arbitrary"), vmem_limit_bytes=VMEM_LIMIT),
        name="mixer_front",
    )(x, w_in, w_rest, rot, bfg, lng, lnb, sgu_w, sgu_bias, avg, pool_w, pool_s)


def _attn_kernel(first_ref, qt_ref, k_ref, vt_ref, o_ref, acc_ref, sa_ref, sb_ref, mta_ref, mtb_ref):
    tq, tk = ATTN_TQ, ATTN_TK
    qi = pl.program_id(2)
    acc_ref[...] = jnp.zeros_like(acc_ref)
    n_sub = pl.num_programs(2) * (tq // tk)
    n_heads = pl.num_programs(1) * ATTN_HEADS
    first = 2 * qi
    for sub in range(tq // tk):
        for h in range(ATTN_HEADS):
            first = jnp.minimum(first, first_ref[((pl.program_id(0) * n_sub + 2 * qi + sub) * n_heads
                                                  + pl.program_id(1) * ATTN_HEADS + h)])
    first_pair = first // 2

    def produce(j, s_ref, mt_ref):
        off = pl.multiple_of(j * tk, tk)
        for h in range(ATTN_HEADS):
            s = jnp.dot(k_ref[0, h, pl.ds(off, tk), :], qt_ref[0, h], preferred_element_type=F32)
            s_ref[h] = s
            mt_ref[h, 0:1, :] = jnp.max(s, axis=0, keepdims=True)

    def consume(j, s_ref, mt_ref, ms, masked):
        new_ms = []
        for h in range(ATTN_HEADS):
            if masked:
                key_pos = lax.broadcasted_iota(jnp.int32, (tk, tq), 0) + j * tk
                qry_pos = lax.broadcasted_iota(jnp.int32, (tk, tq), 1) + qi * tq
                s = jnp.where(key_pos <= qry_pos, s_ref[h], -jnp.inf)
                m_tile = jnp.max(s, axis=0, keepdims=True)
            else:
                s = s_ref[h]
                m_tile = mt_ref[h, 0:1, :]
            m_new = jnp.maximum(ms[h], m_tile)
            alpha = jnp.exp2(ms[h] - m_new)
            p = jnp.exp2(s - m_new).astype(BF16)
            acc_ref[h] = alpha * acc_ref[h] + jnp.dot(vt_ref[0, h, j], p, preferred_element_type=F32)
            new_ms.append(m_new)
        return tuple(new_ms)

    def pair(jj, ms):
        produce(2 * jj + 1, sb_ref, mtb_ref)
        ms = consume(2 * jj, sa_ref, mta_ref, ms, False)
        produce(2 * jj + 2, sa_ref, mta_ref)
        return consume(2 * jj + 1, sb_ref, mtb_ref, ms, False)

    produce(2 * first_pair, sa_ref, mta_ref)
    m0 = jnp.full((1, tq), -jnp.inf, F32)
    ms = lax.fori_loop(first_pair, qi, pair, (m0,) * ATTN_HEADS)
    produce(2 * qi + 1, sb_ref, mtb_ref)
    ms = consume(2 * qi, sa_ref, mta_ref, ms, True)
    consume(2 * qi + 1, sb_ref, mtb_ref, ms, True)

    outs = [(acc_ref[h] / acc_ref[h, HEAD_DIM:HEAD_DIM + 1, :])[0:HEAD_DIM] for h in range(ATTN_HEADS)]
    o_ref[0] = jnp.concatenate(outs, axis=0).T.astype(BF16)


def _causal_attn(qt, k, vt, first_tile):
    bsz, n_heads, seq, _ = k.shape
    assert ATTN_TQ == 2 * ATTN_TK and ATTN_TK == FRONT_ROWS and seq % ATTN_TQ == 0 and n_heads % ATTN_HEADS == 0
    grid_spec = pltpu.PrefetchScalarGridSpec(
        num_scalar_prefetch=1,
        grid=(bsz, n_heads // ATTN_HEADS, seq // ATTN_TQ),
        in_specs=[
            pl.BlockSpec((1, ATTN_HEADS, HEAD_PAD, ATTN_TQ), lambda b, hp, i, first: (b, hp, 0, i)),
            pl.BlockSpec((1, ATTN_HEADS, seq, HEAD_PAD), lambda b, hp, i, first: (b, hp, 0, 0)),
            pl.BlockSpec((1, ATTN_HEADS, seq // ATTN_TK, V_ROWS, ATTN_TK), lambda b, hp, i, first: (b, hp, 0, 0, 0)),
        ],
        out_specs=pl.BlockSpec((1, ATTN_TQ, ATTN_HEADS * HEAD_DIM), lambda b, hp, i, first: (b, i, hp)),
        scratch_shapes=[pltpu.VMEM((ATTN_HEADS, V_ROWS, ATTN_TQ), F32),
                        pltpu.VMEM((ATTN_HEADS, ATTN_TK, ATTN_TQ), F32),
                        pltpu.VMEM((ATTN_HEADS, ATTN_TK, ATTN_TQ), F32),
                        pltpu.VMEM((ATTN_HEADS, 8, ATTN_TQ), F32),
                        pltpu.VMEM((ATTN_HEADS, 8, ATTN_TQ), F32)],
    )
    return pl.pallas_call(
        _attn_kernel,
        grid_spec=grid_spec,
        out_shape=jax.ShapeDtypeStruct((bsz, seq, n_heads * HEAD_DIM), BF16),
        compiler_params=pltpu.CompilerParams(
            dimension_semantics=("parallel", "parallel", "arbitrary"), vmem_limit_bytes=VMEM_LIMIT),
        name="causal_attn",
    )(first_tile, qt, k, vt)


def _out_proj_kernel(ya_ref, yb_ref, yc_ref, yd_ref, x_ref, w_ref, g_ref, b_ref, o_ref):
    half = PROJ_ROWS // 2
    spans = [slice(i * half, (i + 1) * half) for i in range(2)]
    ys = []
    for sp in spans:
        y = jnp.concatenate([ya_ref[sp, :], yb_ref[sp, :], yc_ref[sp, :], yd_ref[sp, :]], axis=1)
        ys.append(jnp.dot(y, w_ref[...], preferred_element_type=F32))
    for sp, y in zip(spans, ys):
        o_ref[sp, :] = _layer_norm_rows(DN_ALPHA * x_ref[sp, :] + y, g_ref[...], b_ref[...])


def _out_proj_ln(ya, yb, yc, yd, x, layer, w_o, g, b):
    m = x.shape[0]
    rows = PROJ_ROWS
    assert m % rows == 0
    part = pl.BlockSpec((rows, GROUP_W), lambda i: (i, 0))
    vec = pl.BlockSpec((None, 1, D_MODEL), lambda i: (layer, 0, 0))
    return pl.pallas_call(
        _out_proj_kernel,
        grid=(m // rows,),
        in_specs=[part, part, part, part,
                  pl.BlockSpec((rows, D_MODEL), lambda i: (i, 0)),
                  pl.BlockSpec((None, D_MODEL, D_MODEL), lambda i: (layer, 0, 0)), vec, vec],
        out_specs=pl.BlockSpec((rows, D_MODEL), lambda i: (i, 0)),
        out_shape=jax.ShapeDtypeStruct((m, D_MODEL), F32),
        compiler_params=pltpu.CompilerParams(dimension_semantics=("parallel",), vmem_limit_bytes=VMEM_LIMIT),
        name="out_proj_ln",
    )(ya, yb, yc, yd, x, w_o, g, b)


def _ffn_kernel(tiles_per_seq, x_ref, xp_ref, wu_ref, cw_ref, cb_ref, wd_ref, g_ref, b_ref, o_ref,
                xb_ref, ha_ref, hb_ref, acc_ref):
    m = pl.program_id(0)
    cols = FFN_COLS
    n_col = D_FF // cols
    prev = jnp.where(m % tiles_per_seq == 0, 0.0, xp_ref[...])
    xb_ref[0:FFN_HALO, :] = prev.astype(BF16)
    xb_ref[FFN_HALO:, :] = x_ref[...].astype(BF16)

    def both(n, ref):
        return jnp.concatenate([ref[:, n * cols:(n + 1) * cols], ref[:, D_FF + n * cols:D_FF + (n + 1) * cols]],
                               axis=1)

    def up(n, h_ref):
        xb = xb_ref[...]
        h_ref[:, 0:cols] = jnp.dot(xb, wu_ref[:, n * cols:(n + 1) * cols], preferred_element_type=F32)
        h_ref[:, cols:2 * cols] = jnp.dot(xb, wu_ref[:, D_FF + n * cols:D_FF + (n + 1) * cols],
                                          preferred_element_type=F32)

    def gated(n, h_ref):
        h = h_ref[...]
        cw = both(n, cw_ref)
        c = (pltpu.roll(h, 2, 0) * cw[0:1, :] + pltpu.roll(h, 1, 0) * cw[1:2, :] + h * cw[2:3, :]
             + both(n, cb_ref))[FFN_HALO:, :]
        gate, val = c[:, 0:cols], c[:, cols:2 * cols]
        return (gate * jax.nn.sigmoid(gate) * val).astype(BF16)

    def down(n, act):
        return jnp.dot(act, wd_ref[n * cols:(n + 1) * cols, :], preferred_element_type=F32)

    bufs = (ha_ref, hb_ref)
    up(0, bufs[0])
    parts = []
    for n in range(n_col):
        if n + 1 < n_col:
            up(n + 1, bufs[(n + 1) % 2])
        parts.append(down(n, gated(n, bufs[n % 2])))
        if len(parts) == 2:
            if n == 1:
                acc_ref[...] = parts[0] + parts[1]
            else:
                acc_ref[...] += parts[0] + parts[1]
            parts = []
    y = acc_ref[...] + parts[0]
    o_ref[...] = _layer_norm_rows(DN_ALPHA * x_ref[...] + y, g_ref[...], b_ref[...])


def _conv_ffn_ln(x, seq, layer, w_up, conv_w, conv_b, w_down, g, b):
    m = x.shape[0]
    rows = FFN_ROWS
    n_col = D_FF // FFN_COLS
    assert m % rows == 0 and seq % rows == 0 and rows % FFN_HALO == 0 and n_col % 2 == 1 and n_col >= 3
    halo_blocks = rows // FFN_HALO
    resident = lambda *shape: pl.BlockSpec((None,) + shape, lambda i: (layer,) + (0,) * len(shape),
                                           pipeline_mode=pl.Buffered(1))
    return pl.pallas_call(
        functools.partial(_ffn_kernel, seq // rows),
        grid=(m // rows,),
        in_specs=[
            pl.BlockSpec((rows, D_MODEL), lambda i: (i, 0)),
            pl.BlockSpec((FFN_HALO, D_MODEL), lambda i: (jnp.maximum(i * halo_blocks - 1, 0), 0)),
            resident(D_MODEL, 2 * D_FF),
            resident(3, 2 * D_FF),
            resident(1, 2 * D_FF),
            resident(D_FF, D_MODEL),
            resident(1, D_MODEL), resident(1, D_MODEL),
        ],
        out_specs=pl.BlockSpec((rows, D_MODEL), lambda i: (i, 0)),
        out_shape=jax.ShapeDtypeStruct((m, D_MODEL), F32),
        scratch_shapes=[pltpu.VMEM((FFN_HALO + rows, D_MODEL), BF16),
                        pltpu.VMEM((FFN_HALO + rows, 2 * FFN_COLS), F32),
                        pltpu.VMEM((FFN_HALO + rows, 2 * FFN_COLS), F32),
                        pltpu.VMEM((rows, D_MODEL), F32)],
        compiler_params=pltpu.CompilerParams(dimension_semantics=("parallel",), vmem_limit_bytes=VMEM_LIMIT),
        name="conv_ffn_ln",
    )(x, x, w_up, conv_w, conv_b, w_down, g, b)


def _rotary_lane_tables(seq):
    pos = jnp.arange(seq, dtype=F32)
    inv_freq = ROPE_THETA ** (-jnp.arange(0, ROPE_DIM, 2, dtype=F32) / ROPE_DIM)
    half = ROPE_DIM // 2
    d = np.arange(PAIR_W) % HEAD_DIM
    freq_lane = jnp.where(d < ROPE_DIM, inv_freq[d % half], 0.0)
    ang = pos[:, None] * freq_lane[None, :]
    cos, sin = jnp.cos(ang), jnp.sin(ang)
    sin_a = jnp.where(d < half, -sin, 0.0)
    sin_b = jnp.where((d >= half) & (d < ROPE_DIM), sin, 0.0)
    return jnp.stack([cos, sin_a, sin_b])


def kernel(x, w_in, b_forget, sgu_ln_g, sgu_ln_b, sgu_w, sgu_b, pool_w, pool_scale, w_o, ln1_g, ln1_b,
           w_up, conv_w, conv_b, w_down, ln2_g, ln2_b):
    bsz, seq, _ = x.shape
    depth = w_in.shape[0]
    row = lambda t: t.reshape(depth, 1, t.shape[-1])
    rot = _rotary_lane_tables(seq)
    group = jnp.arange(GROUP_W) // HEAD_DIM
    avg = jnp.where(group[:, None] == group[None, :], 1.0 / HEAD_DIM, 0.0).astype(BF16)
    f0 = COL_FORGET
    w_qkv = w_in[:, :, :f0].astype(BF16)
    w_rest = jnp.concatenate([w_in[:, :, f0 + N_HEADS:], w_in[:, :, f0:f0 + N_HEADS],
                              jnp.zeros((depth, D_MODEL, 128 - N_HEADS), F32)], axis=-1).astype(BF16)
    bfg = jnp.broadcast_to(jnp.pad(b_forget, ((0, 0), (0, 8 - N_HEADS)))[:, :, None], (depth, 8, 128))
    sgu_bias = jnp.repeat(sgu_b.transpose(0, 2, 1), HEAD_DIM, axis=2)
    n_pool = len(POOL_WINDOWS)
    same_group = jnp.eye(n_pool, dtype=bool)[None, :, None, :, None]
    pool_bd = jnp.where(same_group, pool_w[:, :, :, None, :], 0.0).reshape(
        depth, GROUP_W, GROUP_W).astype(BF16)
    w_o_b, w_up_b, w_down_b = w_o.astype(BF16), w_up.astype(BF16), w_down.astype(BF16)

    xf = x.reshape(bsz * seq, D_MODEL)
    for l in range(depth):
        qa, ka, va, qb, kb, vb, yc, yd, first = _mixer_front(
            xf.reshape(bsz, seq, D_MODEL), l, w_qkv, w_rest, rot, bfg, row(sgu_ln_g), row(sgu_ln_b), sgu_w,
            sgu_bias, avg, pool_bd, row(pool_scale))
        first_b = first[:, :, 0, :N_HEADS].reshape(-1)
        ya = _causal_attn(qa, ka, va, jnp.zeros_like(first_b)).reshape(bsz * seq, GROUP_W)
        yb = _causal_attn(qb, kb, vb, first_b).reshape(bsz * seq, GROUP_W)
        xf = _out_proj_ln(ya, yb, yc.reshape(bsz * seq, GROUP_W), yd.reshape(bsz * seq, GROUP_W), xf, l,
                          w_o_b, row(ln1_g), row(ln1_b))
        xf = _conv_ffn_ln(xf, seq, l, w_up_b, conv_w, row(conv_b), w_down_b, row(ln2_g), row(ln2_b))
    return xf.reshape(bsz, seq, D_MODEL)
```

```python
import functools

import numpy as np
import jax
import jax.numpy as jnp
from jax import lax
from jax.experimental import pallas as pl
from jax.experimental.pallas import tpu as pltpu

F32 = jnp.float32
BF16 = jnp.bfloat16

D_MODEL = 1024
DEPTH = 2
HEAD_DIM = 64
N_HEADS = 4
GROUP_W = 256
HEAD_PAD = 128
PAIR_W = 2 * HEAD_DIM
V_ROWS = 80
MOBA_BLOCK = 256
MOBA_TOPK = 3
ROPE_THETA = 500000.0
ROPE_DIM = HEAD_DIM // 4
SGU_CHUNK = 128
SGU_GROUPS = 4
POOL_WINDOWS = (2, 4, 8, 16)
POOL_HALO = 16
D_FF = 2816
DN_ALPHA = (2 * DEPTH) ** 0.25
LN_EPS = 1e-5
LOG2_E = 1.4426950408889634
QK_SCALE = HEAD_DIM ** -0.5 * LOG2_E
MASK_BIAS = -1e30
SKIP_LOG2 = 160.0
NORM_SLACK = 1.02

FRONT_ROWS = MOBA_BLOCK
ATTN_TQ = 512
ATTN_TK = 256
ATTN_HEADS = 4
PROJ_ROWS = 1024
PROJ_PARTS = 4
FFN_ROWS = 512
FFN_COLS = 256
FFN_HALO = 16
VMEM_LIMIT = 56 * 1024 * 1024

COL_A = 0
COL_B = 3 * GROUP_W
COL_FORGET = 6 * GROUP_W
REST_W = 3 * GROUP_W + 128


def _layer_norm_rows(r, g, b):
    mu = jnp.mean(r, axis=-1, keepdims=True)
    d = r - mu
    var = jnp.mean(d * d, axis=-1, keepdims=True)
    return d * lax.rsqrt(var + LN_EPS) * g + b


def _split_bf16(t):
    hi = t.astype(BF16)
    lo = (t - hi.astype(F32)).astype(BF16)
    return hi, lo


def _group_mean(t, avg):
    hi, lo = _split_bf16(t)
    return (jnp.dot(hi, avg, preferred_element_type=F32) + jnp.dot(lo, avg, preferred_element_type=F32))


def _front_kernel(x_ref, wqkv32_ref, wrest_ref, rot_ref, bfg_ref, lng_ref, lnb_ref, sw_ref, sb_ref,
                  avg_ref, pw_ref, ps_ref,
                  qa_ref, ka_ref, va_ref, qb_ref, kb_ref, vb_ref, yc_ref, yd_ref, first_ref,
                  kbar_ref, carry_ref, halo_ref, hist_ref, wqkv_ref):
    rows = FRONT_ROWS
    i = pl.program_id(1)
    n_blk = kbar_ref.shape[1]

    @pl.when((i == 0) & (pl.program_id(0) == 0))
    def _():
        wqkv_ref[...] = wqkv32_ref[...].astype(BF16)

    @pl.when(i == 0)
    def _():
        kbar_ref[...] = jnp.zeros_like(kbar_ref)
        carry_ref[...] = jnp.zeros_like(carry_ref)
        halo_ref[...] = jnp.zeros_like(halo_ref)
        hist_ref[...] = jnp.zeros_like(hist_ref)

    xb = x_ref[0].astype(BF16)
    lane = lax.broadcasted_iota(jnp.int32, (rows, PAIR_W), 1)
    ones_slab = jnp.where(lax.broadcasted_iota(jnp.int32, (V_ROWS - HEAD_DIM, rows), 0) == 0, 1.0, 0.0)

    def head_rows(pair_t, e):
        return pair_t[e * HEAD_DIM:(e + 1) * HEAD_DIM]

    def head_lanes_first(pair, e):
        return pair if e == 0 else pltpu.roll(pair, HEAD_DIM, 1)

    cv = jnp.dot(xb, wrest_ref[:, GROUP_W:2 * GROUP_W], preferred_element_type=F32)
    cu = jnp.dot(xb, wrest_ref[:, 0:GROUP_W], preferred_element_type=F32)
    dp_fl = jnp.dot(xb, wrest_ref[:, 2 * GROUP_W:REST_W], preferred_element_type=F32)
    za = jnp.dot(xb, wqkv_ref[:, COL_A:COL_A + 3 * GROUP_W], preferred_element_type=F32)
    cos_t, sin_a, sin_b = rot_ref[0], rot_ref[1], rot_ref[2]

    def rotary(t):
        return (t * cos_t + pltpu.roll(t, PAIR_W - ROPE_DIM // 2, 1) * sin_a
                + pltpu.roll(t, ROPE_DIM // 2, 1) * sin_b)

    dp = dp_fl[:, 0:GROUP_W]

    inv_sqrt2 = np.float32(1.0 / np.sqrt(2.0))
    u = 0.5 * cu * (1.0 + lax.erf(cu * inv_sqrt2))
    vg = 0.5 * cv * (1.0 + lax.erf(cv * inv_sqrt2))
    avg = avg_ref[...]
    mu = _group_mean(vg, avg)

    sub8 = lax.broadcasted_iota(jnp.int32, (8, rows), 0)
    fl = dp_fl[:, GROUP_W:GROUP_W + 128].T[0:8, :] + jnp.broadcast_to(bfg_ref[:, 0:1], (8, rows))
    log_f = jnp.minimum(fl, 0.0) - jnp.log1p(jnp.exp(-jnp.abs(fl)))
    log_f = jnp.where(sub8 < N_HEADS, log_f, 0.0)
    r_i = lax.broadcasted_iota(jnp.int32, (rows, rows), 0)
    c_i = lax.broadcasted_iota(jnp.int32, (rows, rows), 1)
    tri = jnp.where(r_i <= c_i, 1.0, 0.0).astype(BF16)
    f_hi = log_f.astype(BF16)
    f_res = log_f - f_hi.astype(F32)
    f_mid = f_res.astype(BF16)
    f_lo = (f_res - f_mid.astype(F32)).astype(BF16)
    cum_t = (jnp.dot(f_hi, tri, preferred_element_type=F32) + jnp.dot(f_mid, tri, preferred_element_type=F32)
             + jnp.dot(f_lo, tri, preferred_element_type=F32) + jnp.broadcast_to(carry_ref[:, 0:1], (8, rows)))
    carry_ref[...] = jnp.broadcast_to(cum_t[:, rows - 1:rows], (8, 128))

    zb = jnp.dot(xb, wqkv_ref[:, COL_B:COL_B + 3 * GROUP_W], preferred_element_type=F32)

    dv = vg - mu
    var = _group_mean(dv * dv, avg)
    vn = dv * lax.rsqrt(var + LN_EPS) * lng_ref[...] + lnb_ref[...]
    lane_w = lax.broadcasted_iota(jnp.int32, (SGU_CHUNK, GROUP_W), 1)
    t_i = lax.broadcasted_iota(jnp.int32, (SGU_CHUNK, SGU_CHUNK), 0)
    s_i = lax.broadcasted_iota(jnp.int32, (SGU_CHUNK, SGU_CHUNK), 1)
    w_tril = [jnp.where(s_i <= t_i, sw_ref[g], 0.0).astype(BF16) for g in range(SGU_GROUPS)]
    for c in range(rows // SGU_CHUNK):
        vn_c = vn[c * SGU_CHUNK:(c + 1) * SGU_CHUNK, :].astype(BF16)
        mixed = sb_ref[...]
        for g in range(SGU_GROUPS):
            mg = jnp.dot(w_tril[g], vn_c, preferred_element_type=F32)
            mixed = mixed + jnp.where(lane_w // HEAD_DIM == g, mg, 0.0)
        yc_ref[0, c * SGU_CHUNK:(c + 1) * SGU_CHUNK, :] = (
            u[c * SGU_CHUNK:(c + 1) * SGU_CHUNK, :] * mixed).astype(BF16)

    ext = jnp.concatenate([halo_ref[...], dp], axis=0)
    s2 = ext + pltpu.roll(ext, 1, 0)
    s4 = s2 + pltpu.roll(s2, 2, 0)
    s8 = s4 + pltpu.roll(s4, 4, 0)
    s16 = s8 + pltpu.roll(s8, 8, 0)
    lane_g = lax.broadcasted_iota(jnp.int32, (rows, GROUP_W), 1) // (GROUP_W // len(POOL_WINDOWS))
    t_glob = lax.broadcasted_iota(jnp.int32, (rows, GROUP_W), 0) + i * rows
    win = jnp.where(lane_g == 0, POOL_WINDOWS[0], jnp.where(lane_g == 1, POOL_WINDOWS[1],
          jnp.where(lane_g == 2, POOL_WINDOWS[2], POOL_WINDOWS[3])))
    wsum = jnp.where(lane_g == 0, s2[POOL_HALO:], jnp.where(lane_g == 1, s4[POOL_HALO:],
           jnp.where(lane_g == 2, s8[POOL_HALO:], s16[POOL_HALO:])))
    count = jnp.minimum(t_glob + 1, win).astype(F32)
    pooled = wsum / count - dp
    yd = jnp.dot(pooled.astype(BF16), pw_ref[...], preferred_element_type=F32) * ps_ref[...]
    yd_ref[0] = yd.astype(BF16)
    halo_ref[...] = dp[rows - POOL_HALO:, :]

    blk_row = lax.broadcasted_iota(jnp.int32, (n_blk, rows), 0)
    blk_lane = lax.broadcasted_iota(jnp.int32, (n_blk, PAIR_W), 1)
    for pr in range(N_HEADS // 2):
        q2 = rotary(za[:, pr * PAIR_W:(pr + 1) * PAIR_W])
        k2 = rotary(za[:, GROUP_W + pr * PAIR_W:GROUP_W + (pr + 1) * PAIR_W])
        q2_t = (q2 * QK_SCALE).T
        v2_t = za[:, 2 * GROUP_W + pr * PAIR_W:2 * GROUP_W + (pr + 1) * PAIR_W].T
        for e in range(2):
            h = 2 * pr + e
            kbar = jnp.where(blk_lane // HEAD_DIM == e, kbar_ref[pr], 0.0)
            gate_t = lax.dot_general(kbar, q2, (((1,), (1,)), ((), ())),
                                     precision=lax.Precision.HIGHEST, preferred_element_type=F32)
            g = jnp.where(blk_row < i, gate_t, -jnp.inf)
            chosen = jnp.zeros((n_blk, rows), F32)
            for _ in range(MOBA_TOPK):
                top = jnp.max(g, axis=0, keepdims=True)
                first = jnp.min(jnp.where(g == top, blk_row, n_blk), axis=0, keepdims=True)
                pick = blk_row == first
                chosen = jnp.where(pick, 1.0, chosen)
                g = jnp.where(pick, -jnp.inf, g)
            keep = jnp.where(blk_row < i, chosen, jnp.where(blk_row == i, 1.0, 0.0))
            bias_t = jnp.where(keep > 0.0, 0.0, MASK_BIAS)
            qa_ref[0, h] = jnp.concatenate(
                [head_rows(q2_t, e), bias_t, jnp.zeros((HEAD_PAD - HEAD_DIM - n_blk, rows), F32)],
                axis=0).astype(BF16)
            ka_ref[0, h] = jnp.where(lane < HEAD_DIM, head_lanes_first(k2, e),
                                     jnp.where(lane == HEAD_DIM + i, 1.0, 0.0)).astype(BF16)
            va_ref[0, h, 0] = jnp.concatenate([head_rows(v2_t, e), ones_slab], axis=0).astype(BF16)
        kbar_ref[pr, pl.ds(i, 1), :] = jnp.sum(k2, axis=0, keepdims=True) * (1.0 / MOBA_BLOCK)

    cum_t = cum_t * LOG2_E
    cum = jnp.concatenate([cum_t, jnp.zeros((128 - 8, rows), F32)], axis=0).T
    tile_lane = lax.broadcasted_iota(jnp.int32, (1, 128), 1)
    first_needed = jnp.zeros((1, 128), jnp.int32)
    for pr in range(N_HEADS // 2):
        q2_t = (zb[:, pr * PAIR_W:(pr + 1) * PAIR_W] * QK_SCALE).T
        k2 = zb[:, GROUP_W + pr * PAIR_W:GROUP_W + (pr + 1) * PAIR_W]
        v2_t = zb[:, 2 * GROUP_W + pr * PAIR_W:2 * GROUP_W + (pr + 1) * PAIR_W].T
        for e in range(2):
            h = 2 * pr + e
            q_norm = jnp.sqrt(jnp.max(jnp.sum(jnp.square(head_rows(q2_t, e)), axis=0, keepdims=True),
                                      axis=1, keepdims=True)) * NORM_SLACK
            k_sq = jnp.where(lane // HEAD_DIM == e, jnp.square(k2), 0.0)
            k_norm = jnp.sqrt(jnp.max(jnp.sum(k_sq, axis=1, keepdims=True), axis=0, keepdims=True))
            hist_ref[h:h + 1, :] = jnp.where(tile_lane == i, k_norm, hist_ref[h:h + 1, :])
            hist_ref[N_HEADS + h:N_HEADS + h + 1, :] = jnp.where(
                tile_lane == i, cum[rows - 1:rows, h:h + 1], hist_ref[N_HEADS + h:N_HEADS + h + 1, :])
            upper = q_norm * hist_ref[h:h + 1, :] + cum[0:1, h:h + 1] - hist_ref[N_HEADS + h:N_HEADS + h + 1, :]
            skippable = (upper < -(q_norm * k_norm) - SKIP_LOG2) & (tile_lane < i)
            first_h = jnp.min(jnp.where(skippable, 128, tile_lane), axis=1, keepdims=True)
            first_needed = jnp.where(tile_lane == h, first_h, first_needed)
            c = jnp.broadcast_to(cum[:, h:h + 1], (rows, PAIR_W))
            hi = c.astype(BF16).astype(F32)
            mid = (c - hi).astype(BF16).astype(F32)
            lo = c - hi - mid
            k_aug = jnp.where(lane < HEAD_DIM, head_lanes_first(k2, e),
                    jnp.where(lane < HEAD_DIM + 3, 1.0,
                    jnp.where(lane == HEAD_DIM + 3, -hi,
                    jnp.where(lane == HEAD_DIM + 4, -mid,
                    jnp.where(lane == HEAD_DIM + 5, -lo, 0.0)))))
            c_t = jnp.broadcast_to(cum_t[h:h + 1, :], (8, rows))
            hi_t = c_t.astype(BF16).astype(F32)
            mid_t = (c_t - hi_t).astype(BF16).astype(F32)
            lo_t = c_t - hi_t - mid_t
            q_bias = jnp.where(sub8 == 0, hi_t, jnp.where(sub8 == 1, mid_t, jnp.where(sub8 == 2, lo_t,
                     jnp.where(sub8 < 6, 1.0, 0.0))))
            qb_ref[0, h] = jnp.concatenate(
                [head_rows(q2_t, e), q_bias, jnp.zeros((HEAD_PAD - HEAD_DIM - 8, rows), F32)],
                axis=0).astype(BF16)
            kb_ref[0, h] = k_aug.astype(BF16)
            vb_ref[0, h, 0] = jnp.concatenate([head_rows(v2_t, e), ones_slab], axis=0).astype(BF16)
    first_ref[0, 0] = jnp.broadcast_to(first_needed, (8, 128))


def _mixer_front(x, layer, w_in, w_rest, rot, bfg, lng, lnb, sgu_w, sgu_bias, avg, pool_w, pool_s):
    bsz, seq, _ = x.shape
    n_blk = seq // MOBA_BLOCK
    assert seq % ATTN_TK == 0 and ATTN_TK % FRONT_ROWS == 0
    assert n_blk <= HEAD_PAD - HEAD_DIM and n_blk % 8 == 0 and seq // FRONT_ROWS <= 128
    rows = FRONT_ROWS
    const = lambda *shape: pl.BlockSpec(shape, lambda b, i: (0,) * len(shape))
    of_layer = lambda *shape: pl.BlockSpec((None,) + shape, lambda b, i: (layer,) + (0,) * len(shape))
    per_tk = ATTN_TK // rows
    q_spec = pl.BlockSpec((1, N_HEADS, HEAD_PAD, rows), lambda b, i: (b, 0, 0, i))
    k_spec = pl.BlockSpec((1, N_HEADS, rows, HEAD_PAD), lambda b, i: (b, 0, i, 0))
    v_spec = pl.BlockSpec((1, N_HEADS, 1, V_ROWS, rows), lambda b, i: (b, 0, i // per_tk, 0, i % per_tk))
    flat_spec = pl.BlockSpec((1, rows, GROUP_W), lambda b, i: (b, i, 0))
    q_shape = jax.ShapeDtypeStruct((bsz, N_HEADS, HEAD_PAD, seq), BF16)
    k_shape = jax.ShapeDtypeStruct((bsz, N_HEADS, seq, HEAD_PAD), BF16)
    v_shape = jax.ShapeDtypeStruct((bsz, N_HEADS, seq // ATTN_TK, V_ROWS, ATTN_TK), BF16)
    flat_shape = jax.ShapeDtypeStruct((bsz, seq, GROUP_W), BF16)
    return pl.pallas_call(
        _front_kernel,
        grid=(bsz, seq // rows),
        in_specs=[
            pl.BlockSpec((1, rows, D_MODEL), lambda b, i: (b, i, 0)),
            pl.BlockSpec((None, D_MODEL, COL_FORGET), lambda b, i: (layer, 0, 0),
                         pipeline_mode=pl.Buffered(1)),
            of_layer(D_MODEL, REST_W),
            pl.BlockSpec((3, rows, PAIR_W), lambda b, i: (0, i, 0)),
            of_layer(8, 128), of_layer(1, GROUP_W), of_layer(1, GROUP_W),
            of_layer(SGU_GROUPS, SGU_CHUNK, SGU_CHUNK), of_layer(SGU_CHUNK, GROUP_W),
            const(GROUP_W, GROUP_W), of_layer(GROUP_W, GROUP_W), of_layer(1, GROUP_W),
        ],
        out_specs=[q_spec, k_spec, v_spec] * 2 + [flat_spec] * 2 + [
            pl.BlockSpec((1, 1, 8, 128), lambda b, i: (b, i, 0, 0))],
        out_shape=[q_shape, k_shape, v_shape] * 2 + [flat_shape] * 2 + [
            jax.ShapeDtypeStruct((bsz, seq // rows, 8, 128), jnp.int32)],
        scratch_shapes=[
            pltpu.VMEM((N_HEADS // 2, n_blk, PAIR_W), F32),
            pltpu.VMEM((8, 128), F32),
            pltpu.VMEM((POOL_HALO, GROUP_W), F32),
            pltpu.VMEM((2 * N_HEADS, 128), F32),
            pltpu.VMEM((D_MODEL, COL_FORGET), BF16),
        ],
        compiler_params=pltpu.CompilerParams(
            dimension_semantics=("arbitrary", "arbitrary"), vmem_limit_bytes=VMEM_LIMIT),
        name="mixer_front",
    )(x, w_in, w_rest, rot, bfg, lng, lnb, sgu_w, sgu_bias, avg, pool_w, pool_s)


def _attn_kernel(first_ref, qt_ref, k_ref, vt_ref, o_ref, acc_ref, sa_ref, sb_ref, mta_ref, mtb_ref):
    tq, tk = ATTN_TQ, ATTN_TK
    qi = pl.program_id(2)
    acc_ref[...] = jnp.zeros_like(acc_ref)
    n_sub = pl.num_programs(2) * (tq // tk)
    n_heads = pl.num_programs(1) * ATTN_HEADS
    first = 2 * qi
    for sub in range(tq // tk):
        for h in range(ATTN_HEADS):
            first = jnp.minimum(first, first_ref[((pl.program_id(0) * n_sub + 2 * qi + sub) * n_heads
                                                  + pl.program_id(1) * ATTN_HEADS + h)])
    first_pair = first // 2

    def produce(j, s_ref, mt_ref):
        off = pl.multiple_of(j * tk, tk)
        for h in range(ATTN_HEADS):
            s = jnp.dot(k_ref[0, h, pl.ds(off, tk), :], qt_ref[0, h], preferred_element_type=F32)
            s_ref[h] = s
            mt_ref[h, 0:1, :] = jnp.max(s, axis=0, keepdims=True)

    def consume(j, s_ref, mt_ref, ms, masked):
        new_ms = []
        for h in range(ATTN_HEADS):
            if masked:
                key_pos = lax.broadcasted_iota(jnp.int32, (tk, tq), 0) + j * tk
                qry_pos = lax.broadcasted_iota(jnp.int32, (tk, tq), 1) + qi * tq
                s = jnp.where(key_pos <= qry_pos, s_ref[h], -jnp.inf)
                m_tile = jnp.max(s, axis=0, keepdims=True)
            else:
                s = s_ref[h]
                m_tile = mt_ref[h, 0:1, :]
            m_new = jnp.maximum(ms[h], m_tile)
            alpha = jnp.exp2(ms[h] - m_new)
            p = jnp.exp2(s - m_new).astype(BF16)
            acc_ref[h] = alpha * acc_ref[h] + jnp.dot(vt_ref[0, h, j], p, preferred_element_type=F32)
            new_ms.append(m_new)
        return tuple(new_ms)

    def pair(jj, ms):
        produce(2 * jj + 1, sb_ref, mtb_ref)
        ms = consume(2 * jj, sa_ref, mta_ref, ms, False)
        produce(2 * jj + 2, sa_ref, mta_ref)
        return consume(2 * jj + 1, sb_ref, mtb_ref, ms, False)

    produce(2 * first_pair, sa_ref, mta_ref)
    m0 = jnp.full((1, tq), -jnp.inf, F32)
    ms = lax.fori_loop(first_pair, qi, pair, (m0,) * ATTN_HEADS)
    produce(2 * qi + 1, sb_ref, mtb_ref)
    ms = consume(2 * qi, sa_ref, mta_ref, ms, True)
    consume(2 * qi + 1, sb_ref, mtb_ref, ms, True)

    outs = [(acc_ref[h] / acc_ref[h, HEAD_DIM:HEAD_DIM + 1, :])[0:HEAD_DIM] for h in range(ATTN_HEADS)]
    o_ref[0] = jnp.concatenate(outs, axis=0).T.astype(BF16)


def _causal_attn(qt, k, vt, first_tile):
    bsz, n_heads, seq, _ = k.shape
    assert ATTN_TQ == 2 * ATTN_TK and ATTN_TK == FRONT_ROWS and seq % ATTN_TQ == 0 and n_heads % ATTN_HEADS == 0
    grid_spec = pltpu.PrefetchScalarGridSpec(
        num_scalar_prefetch=1,
        grid=(bsz, n_heads // ATTN_HEADS, seq // ATTN_TQ),
        in_specs=[
            pl.BlockSpec((1, ATTN_HEADS, HEAD_PAD, ATTN_TQ), lambda b, hp, i, first: (b, hp, 0, i)),
            pl.BlockSpec((1, ATTN_HEADS, seq, HEAD_PAD), lambda b, hp, i, first: (b, hp, 0, 0)),
            pl.BlockSpec((1, ATTN_HEADS, seq // ATTN_TK, V_ROWS, ATTN_TK), lambda b, hp, i, first: (b, hp, 0, 0, 0)),
        ],
        out_specs=pl.BlockSpec((1, ATTN_TQ, ATTN_HEADS * HEAD_DIM), lambda b, hp, i, first: (b, i, hp)),
        scratch_shapes=[pltpu.VMEM((ATTN_HEADS, V_ROWS, ATTN_TQ), F32),
                        pltpu.VMEM((ATTN_HEADS, ATTN_TK, ATTN_TQ), F32),
                        pltpu.VMEM((ATTN_HEADS, ATTN_TK, ATTN_TQ), F32),
                        pltpu.VMEM((ATTN_HEADS, 8, ATTN_TQ), F32),
                        pltpu.VMEM((ATTN_HEADS, 8, ATTN_TQ), F32)],
    )
    return pl.pallas_call(
        _attn_kernel,
        grid_spec=grid_spec,
        out_shape=jax.ShapeDtypeStruct((bsz, seq, n_heads * HEAD_DIM), BF16),
        compiler_params=pltpu.CompilerParams(
            dimension_semantics=("parallel", "parallel", "arbitrary"), vmem_limit_bytes=VMEM_LIMIT),
        name="causal_attn",
    )(first_tile, qt, k, vt)


def _out_proj_kernel(ya_ref, yb_ref, yc_ref, yd_ref, x_ref, w_ref, g_ref, b_ref, o_ref):
    part = PROJ_ROWS // PROJ_PARTS
    spans = [slice(k * part, (k + 1) * part) for k in range(PROJ_PARTS)]

    def project(sp):
        y = jnp.concatenate([ya_ref[sp, :], yb_ref[sp, :], yc_ref[sp, :], yd_ref[sp, :]], axis=1)
        return jnp.dot(y, w_ref[...], preferred_element_type=F32)

    pending = project(spans[0])
    for k, sp in enumerate(spans):
        y = pending
        if k + 1 < PROJ_PARTS:
            pending = project(spans[k + 1])
        o_ref[sp, :] = _layer_norm_rows(DN_ALPHA * x_ref[sp, :] + y, g_ref[...], b_ref[...])


def _out_proj_ln(ya, yb, yc, yd, x, layer, w_o, g, b):
    m = x.shape[0]
    rows = PROJ_ROWS
    assert m % rows == 0
    part = pl.BlockSpec((rows, GROUP_W), lambda i: (i, 0))
    vec = pl.BlockSpec((None, 1, D_MODEL), lambda i: (layer, 0, 0))
    return pl.pallas_call(
        _out_proj_kernel,
        grid=(m // rows,),
        in_specs=[part, part, part, part,
                  pl.BlockSpec((rows, D_MODEL), lambda i: (i, 0)),
                  pl.BlockSpec((None, D_MODEL, D_MODEL), lambda i: (layer, 0, 0)), vec, vec],
        out_specs=pl.BlockSpec((rows, D_MODEL), lambda i: (i, 0)),
        out_shape=jax.ShapeDtypeStruct((m, D_MODEL), F32),
        compiler_params=pltpu.CompilerParams(dimension_semantics=("parallel",), vmem_limit_bytes=VMEM_LIMIT),
        name="out_proj_ln",
    )(ya, yb, yc, yd, x, w_o, g, b)


def _ffn_kernel(tiles_per_seq, x_ref, xp_ref, wu_ref, cw_ref, cb_ref, wd_ref, g_ref, b_ref, o_ref,
                xb_ref, ha_ref, hb_ref, acc_ref):
    m = pl.program_id(0)
    cols = FFN_COLS
    n_col = D_FF // cols
    prev = jnp.where(m % tiles_per_seq == 0, 0.0, xp_ref[...])
    xb_ref[0:FFN_HALO, :] = prev.astype(BF16)
    xb_ref[FFN_HALO:, :] = x_ref[...].astype(BF16)

    def both(n, ref):
        return jnp.concatenate([ref[:, n * cols:(n + 1) * cols], ref[:, D_FF + n * cols:D_FF + (n + 1) * cols]],
                               axis=1)

    def up(n, h_ref):
        xb = xb_ref[...]
        h_ref[:, 0:cols] = jnp.dot(xb, wu_ref[:, n * cols:(n + 1) * cols], preferred_element_type=F32)
        h_ref[:, cols:2 * cols] = jnp.dot(xb, wu_ref[:, D_FF + n * cols:D_FF + (n + 1) * cols],
                                          preferred_element_type=F32)

    def gated(n, h_ref):
        h = h_ref[...]
        cw = both(n, cw_ref)
        c = (pltpu.roll(h, 2, 0) * cw[0:1, :] + pltpu.roll(h, 1, 0) * cw[1:2, :] + h * cw[2:3, :]
             + both(n, cb_ref))[FFN_HALO:, :]
        gate, val = c[:, 0:cols], c[:, cols:2 * cols]
        return (gate * jax.nn.sigmoid(gate) * val).astype(BF16)

    def down(n, act):
        return jnp.dot(act, wd_ref[n * cols:(n + 1) * cols, :], preferred_element_type=F32)

    bufs = (ha_ref, hb_ref)
    up(0, bufs[0])
    parts = []
    for n in range(n_col):
        if n + 1 < n_col:
            up(n + 1, bufs[(n + 1) % 2])
        parts.append(down(n, gated(n, bufs[n % 2])))
        if len(parts) == 2:
            if n == 1:
                acc_ref[...] = parts[0] + parts[1]
            else:
                acc_ref[...] += parts[0] + parts[1]
            parts = []
    y = acc_ref[...] + parts[0]
    o_ref[...] = _layer_norm_rows(DN_ALPHA * x_ref[...] + y, g_ref[...], b_ref[...])


def _conv_ffn_ln(x, seq, layer, w_up, conv_w, conv_b, w_down, g, b):
    m = x.shape[0]
    rows = FFN_ROWS
    n_col = D_FF // FFN_COLS
    assert m % rows == 0 and seq % rows == 0 and rows % FFN_HALO == 0 and n_col % 2 == 1 and n_col >= 3
    halo_blocks = rows // FFN_HALO
    resident = lambda *shape: pl.BlockSpec((None,) + shape, lambda i: (layer,) + (0,) * len(shape),
                                           pipeline_mode=pl.Buffered(1))
    return pl.pallas_call(
        functools.partial(_ffn_kernel, seq // rows),
        grid=(m // rows,),
        in_specs=[
            pl.BlockSpec((rows, D_MODEL), lambda i: (i, 0)),
            pl.BlockSpec((FFN_HALO, D_MODEL), lambda i: (jnp.maximum(i * halo_blocks - 1, 0), 0)),
            resident(D_MODEL, 2 * D_FF),
            resident(3, 2 * D_FF),
            resident(1, 2 * D_FF),
            resident(D_FF, D_MODEL),
            resident(1, D_MODEL), resident(1, D_MODEL),
        ],
        out_specs=pl.BlockSpec((rows, D_MODEL), lambda i: (i, 0)),
        out_shape=jax.ShapeDtypeStruct((m, D_MODEL), F32),
        scratch_shapes=[pltpu.VMEM((FFN_HALO + rows, D_MODEL), BF16),
                        pltpu.VMEM((FFN_HALO + rows, 2 * FFN_COLS), F32),
                        pltpu.VMEM((FFN_HALO + rows, 2 * FFN_COLS), F32),
                        pltpu.VMEM((rows, D_MODEL), F32)],
        compiler_params=pltpu.CompilerParams(dimension_semantics=("parallel",), vmem_limit_bytes=VMEM_LIMIT),
        name="conv_ffn_ln",
    )(x, x, w_up, conv_w, conv_b, w_down, g, b)


def _rotary_lane_tables(seq):
    pos = jnp.arange(seq, dtype=F32)
    inv_freq = ROPE_THETA ** (-jnp.arange(0, ROPE_DIM, 2, dtype=F32) / ROPE_DIM)
    half = ROPE_DIM // 2
    d = np.arange(PAIR_W) % HEAD_DIM
    freq_lane = jnp.where(d < ROPE_DIM, inv_freq[d % half], 0.0)
    ang = pos[:, None] * freq_lane[None, :]
    cos, sin = jnp.cos(ang), jnp.sin(ang)
    sin_a = jnp.where(d < half, -sin, 0.0)
    sin_b = jnp.where((d >= half) & (d < ROPE_DIM), sin, 0.0)
    return jnp.stack([cos, sin_a, sin_b])


def kernel(x, w_in, b_forget, sgu_ln_g, sgu_ln_b, sgu_w, sgu_b, pool_w, pool_scale, w_o, ln1_g, ln1_b,
           w_up, conv_w, conv_b, w_down, ln2_g, ln2_b):
    bsz, seq, _ = x.shape
    depth = w_in.shape[0]
    row = lambda t: t.reshape(depth, 1, t.shape[-1])
    rot = _rotary_lane_tables(seq)
    group = jnp.arange(GROUP_W) // HEAD_DIM
    avg = jnp.where(group[:, None] == group[None, :], 1.0 / HEAD_DIM, 0.0).astype(BF16)
    f0 = COL_FORGET
    w_rest = jnp.concatenate([w_in[:, :, f0 + N_HEADS:], w_in[:, :, f0:f0 + N_HEADS],
                              jnp.zeros((depth, D_MODEL, 128 - N_HEADS), F32)], axis=-1).astype(BF16)
    bfg = jnp.broadcast_to(jnp.pad(b_forget, ((0, 0), (0, 8 - N_HEADS)))[:, :, None], (depth, 8, 128))
    sgu_bias = jnp.repeat(sgu_b.transpose(0, 2, 1), HEAD_DIM, axis=2)
    n_pool = len(POOL_WINDOWS)
    same_group = jnp.eye(n_pool, dtype=bool)[None, :, None, :, None]
    pool_bd = jnp.where(same_group, pool_w[:, :, :, None, :], 0.0).reshape(
        depth, GROUP_W, GROUP_W).astype(BF16)
    w_o_b, w_up_b, w_down_b = w_o.astype(BF16), w_up.astype(BF16), w_down.astype(BF16)

    xf = x.reshape(bsz * seq, D_MODEL)
    for l in range(depth):
        qa, ka, va, qb, kb, vb, yc, yd, first = _mixer_front(
            xf.reshape(bsz, seq, D_MODEL), l, w_in, w_rest, rot, bfg, row(sgu_ln_g), row(sgu_ln_b), sgu_w,
            sgu_bias, avg, pool_bd, row(pool_scale))
        first_b = first[:, :, 0, :N_HEADS].reshape(-1)
        ya = _causal_attn(qa, ka, va, jnp.zeros_like(first_b)).reshape(bsz * seq, GROUP_W)
        yb = _causal_attn(qb, kb, vb, first_b).reshape(bsz * seq, GROUP_W)
        xf = _out_proj_ln(ya, yb, yc.reshape(bsz * seq, GROUP_W), yd.reshape(bsz * seq, GROUP_W), xf, l,
                          w_o_b, row(ln1_g), row(ln1_b))
        xf = _conv_ffn_ln(xf, seq, l, w_up_b, conv_w, row(conv_b), w_down_b, row(ln2_g), row(ln2_b))
    return xf.reshape(bsz, seq, D_MODEL)
```

```python
import functools

import numpy as np
import jax
import jax.numpy as jnp
from jax import lax
from jax.experimental import pallas as pl
from jax.experimental.pallas import tpu as pltpu

F32 = jnp.float32
BF16 = jnp.bfloat16

D_MODEL = 1024
DEPTH = 2
HEAD_DIM = 64
N_HEADS = 4
GROUP_W = 256
HEAD_PAD = 128
PAIR_W = 2 * HEAD_DIM
V_ROWS = 80
MOBA_BLOCK = 256
MOBA_TOPK = 3
ROPE_THETA = 500000.0
ROPE_DIM = HEAD_DIM // 4
SGU_CHUNK = 128
SGU_GROUPS = 4
POOL_WINDOWS = (2, 4, 8, 16)
POOL_HALO = 16
D_FF = 2816
DN_ALPHA = (2 * DEPTH) ** 0.25
LN_EPS = 1e-5
LOG2_E = 1.4426950408889634
QK_SCALE = HEAD_DIM ** -0.5 * LOG2_E
MASK_BIAS = -1e30
SKIP_LOG2 = 160.0
NORM_SLACK = 1.02

FRONT_ROWS = MOBA_BLOCK
ATTN_TQ = 512
ATTN_TK = 256
ATTN_HEADS = 4
PROJ_ROWS = 1024
PROJ_PARTS = 4
FFN_ROWS = 512
FFN_COLS = 256
FFN_HALO = 16
VMEM_LIMIT = 56 * 1024 * 1024

COL_A = 0
COL_B = 3 * GROUP_W
COL_FORGET = 6 * GROUP_W
REST_W = 3 * GROUP_W + 128


def _layer_norm_rows(r, g, b):
    mu = jnp.mean(r, axis=-1, keepdims=True)
    d = r - mu
    var = jnp.mean(d * d, axis=-1, keepdims=True)
    return d * lax.rsqrt(var + LN_EPS) * g + b


def _split_bf16(t):
    hi = t.astype(BF16)
    lo = (t - hi.astype(F32)).astype(BF16)
    return hi, lo


def _group_mean(t, avg):
    hi, lo = _split_bf16(t)
    return (jnp.dot(hi, avg, preferred_element_type=F32) + jnp.dot(lo, avg, preferred_element_type=F32))


def _front_kernel(x_ref, wqkv32_ref, wrest_ref, rot_ref, bfg_ref, lng_ref, lnb_ref, sw_ref, sb_ref,
                  avg_ref, pw_ref, ps_ref,
                  qa_ref, ka_ref, va_ref, qb_ref, kb_ref, vb_ref, yc_ref, yd_ref, first_ref,
                  kbar_ref, carry_ref, halo_ref, hist_ref, wqkv_ref):
    rows = FRONT_ROWS
    i = pl.program_id(1)
    n_blk = kbar_ref.shape[1]

    @pl.when((i == 0) & (pl.program_id(0) == 0))
    def _():
        wqkv_ref[...] = wqkv32_ref[...].astype(BF16)

    @pl.when(i == 0)
    def _():
        kbar_ref[...] = jnp.zeros_like(kbar_ref)
        carry_ref[...] = jnp.zeros_like(carry_ref)
        halo_ref[...] = jnp.zeros_like(halo_ref)
        hist_ref[...] = jnp.zeros_like(hist_ref)

    xb = x_ref[0].astype(BF16)
    lane = lax.broadcasted_iota(jnp.int32, (rows, PAIR_W), 1)
    ones_slab = jnp.where(lax.broadcasted_iota(jnp.int32, (V_ROWS - HEAD_DIM, rows), 0) == 0, 1.0, 0.0)

    def head_rows(pair_t, e):
        return pair_t[e * HEAD_DIM:(e + 1) * HEAD_DIM]

    def head_lanes_first(pair, e):
        return pair if e == 0 else pltpu.roll(pair, HEAD_DIM, 1)

    cv = jnp.dot(xb, wrest_ref[:, GROUP_W:2 * GROUP_W], preferred_element_type=F32)
    cu = jnp.dot(xb, wrest_ref[:, 0:GROUP_W], preferred_element_type=F32)
    dp_fl = jnp.dot(xb, wrest_ref[:, 2 * GROUP_W:REST_W], preferred_element_type=F32)
    za = jnp.dot(xb, wqkv_ref[:, COL_A:COL_A + 3 * GROUP_W], preferred_element_type=F32)
    cos_t, sin_a, sin_b = rot_ref[0], rot_ref[1], rot_ref[2]

    def rotary(t):
        return (t * cos_t + pltpu.roll(t, PAIR_W - ROPE_DIM // 2, 1) * sin_a
                + pltpu.roll(t, ROPE_DIM // 2, 1) * sin_b)

    dp = dp_fl[:, 0:GROUP_W]

    inv_sqrt2 = np.float32(1.0 / np.sqrt(2.0))
    u = 0.5 * cu * (1.0 + lax.erf(cu * inv_sqrt2))
    vg = 0.5 * cv * (1.0 + lax.erf(cv * inv_sqrt2))
    avg = avg_ref[...]
    mu = _group_mean(vg, avg)

    sub8 = lax.broadcasted_iota(jnp.int32, (8, rows), 0)
    fl = dp_fl[:, GROUP_W:GROUP_W + 128].T[0:8, :] + jnp.broadcast_to(bfg_ref[:, 0:1], (8, rows))
    log_f = jnp.minimum(fl, 0.0) - jnp.log1p(jnp.exp(-jnp.abs(fl)))
    log_f = jnp.where(sub8 < N_HEADS, log_f, 0.0)
    r_i = lax.broadcasted_iota(jnp.int32, (rows, rows), 0)
    c_i = lax.broadcasted_iota(jnp.int32, (rows, rows), 1)
    tri = jnp.where(r_i <= c_i, 1.0, 0.0).astype(BF16)
    f_hi = log_f.astype(BF16)
    f_res = log_f - f_hi.astype(F32)
    f_mid = f_res.astype(BF16)
    f_lo = (f_res - f_mid.astype(F32)).astype(BF16)
    cum_t = (jnp.dot(f_hi, tri, preferred_element_type=F32) + jnp.dot(f_mid, tri, preferred_element_type=F32)
             + jnp.dot(f_lo, tri, preferred_element_type=F32) + jnp.broadcast_to(carry_ref[:, 0:1], (8, rows)))
    carry_ref[...] = jnp.broadcast_to(cum_t[:, rows - 1:rows], (8, 128))

    zb = jnp.dot(xb, wqkv_ref[:, COL_B:COL_B + 3 * GROUP_W], preferred_element_type=F32)

    dv = vg - mu
    var = _group_mean(dv * dv, avg)
    vn = dv * lax.rsqrt(var + LN_EPS) * lng_ref[...] + lnb_ref[...]
    lane_w = lax.broadcasted_iota(jnp.int32, (SGU_CHUNK, GROUP_W), 1)
    t_i = lax.broadcasted_iota(jnp.int32, (SGU_CHUNK, SGU_CHUNK), 0)
    s_i = lax.broadcasted_iota(jnp.int32, (SGU_CHUNK, SGU_CHUNK), 1)
    w_tril = [jnp.where(s_i <= t_i, sw_ref[g], 0.0).astype(BF16) for g in range(SGU_GROUPS)]
    for c in range(rows // SGU_CHUNK):
        vn_c = vn[c * SGU_CHUNK:(c + 1) * SGU_CHUNK, :].astype(BF16)
        mixed = sb_ref[...]
        for g in range(SGU_GROUPS):
            mg = jnp.dot(w_tril[g], vn_c, preferred_element_type=F32)
            mixed = mixed + jnp.where(lane_w // HEAD_DIM == g, mg, 0.0)
        yc_ref[0, c * SGU_CHUNK:(c + 1) * SGU_CHUNK, :] = (
            u[c * SGU_CHUNK:(c + 1) * SGU_CHUNK, :] * mixed).astype(BF16)

    ext = jnp.concatenate([halo_ref[...], dp], axis=0)
    s2 = ext + pltpu.roll(ext, 1, 0)
    s4 = s2 + pltpu.roll(s2, 2, 0)
    s8 = s4 + pltpu.roll(s4, 4, 0)
    s16 = s8 + pltpu.roll(s8, 8, 0)
    lane_g = lax.broadcasted_iota(jnp.int32, (rows, GROUP_W), 1) // (GROUP_W // len(POOL_WINDOWS))
    t_glob = lax.broadcasted_iota(jnp.int32, (rows, GROUP_W), 0) + i * rows
    win = jnp.where(lane_g == 0, POOL_WINDOWS[0], jnp.where(lane_g == 1, POOL_WINDOWS[1],
          jnp.where(lane_g == 2, POOL_WINDOWS[2], POOL_WINDOWS[3])))
    wsum = jnp.where(lane_g == 0, s2[POOL_HALO:], jnp.where(lane_g == 1, s4[POOL_HALO:],
           jnp.where(lane_g == 2, s8[POOL_HALO:], s16[POOL_HALO:])))
    count = jnp.minimum(t_glob + 1, win).astype(F32)
    pooled = wsum / count - dp
    yd = jnp.dot(pooled.astype(BF16), pw_ref[...], preferred_element_type=F32) * ps_ref[...]
    yd_ref[0] = yd.astype(BF16)
    halo_ref[...] = dp[rows - POOL_HALO:, :]

    blk_row = lax.broadcasted_iota(jnp.int32, (n_blk, rows), 0)
    blk_lane = lax.broadcasted_iota(jnp.int32, (n_blk, PAIR_W), 1)
    for pr in range(N_HEADS // 2):
        q2 = rotary(za[:, pr * PAIR_W:(pr + 1) * PAIR_W])
        k2 = rotary(za[:, GROUP_W + pr * PAIR_W:GROUP_W + (pr + 1) * PAIR_W])
        q2_t = (q2 * QK_SCALE).T
        v2_t = za[:, 2 * GROUP_W + pr * PAIR_W:2 * GROUP_W + (pr + 1) * PAIR_W].T
        for e in range(2):
            h = 2 * pr + e
            kbar = jnp.where(blk_lane // HEAD_DIM == e, kbar_ref[pr], 0.0)
            gate_t = lax.dot_general(kbar, q2, (((1,), (1,)), ((), ())),
                                     precision=lax.Precision.HIGHEST, preferred_element_type=F32)
            g = jnp.where(blk_row < i, gate_t, -jnp.inf)
            chosen = jnp.zeros((n_blk, rows), F32)
            for _ in range(MOBA_TOPK):
                top = jnp.max(g, axis=0, keepdims=True)
                first = jnp.min(jnp.where(g == top, blk_row, n_blk), axis=0, keepdims=True)
                pick = blk_row == first
                chosen = jnp.where(pick, 1.0, chosen)
                g = jnp.where(pick, -jnp.inf, g)
            keep = jnp.where(blk_row < i, chosen, jnp.where(blk_row == i, 1.0, 0.0))
            bias_t = jnp.where(keep > 0.0, 0.0, MASK_BIAS)
            qa_ref[0, h] = jnp.concatenate(
                [head_rows(q2_t, e), bias_t, jnp.zeros((HEAD_PAD - HEAD_DIM - n_blk, rows), F32)],
                axis=0).astype(BF16)
            ka_ref[0, h] = jnp.where(lane < HEAD_DIM, head_lanes_first(k2, e),
                                     jnp.where(lane == HEAD_DIM + i, 1.0, 0.0)).astype(BF16)
            va_ref[0, h, 0] = jnp.concatenate([head_rows(v2_t, e), ones_slab], axis=0).astype(BF16)
        kbar_ref[pr, pl.ds(i, 1), :] = jnp.sum(k2, axis=0, keepdims=True) * (1.0 / MOBA_BLOCK)

    cum_t = cum_t * LOG2_E
    cum = jnp.concatenate([cum_t, jnp.zeros((128 - 8, rows), F32)], axis=0).T
    tile_lane = lax.broadcasted_iota(jnp.int32, (1, 128), 1)
    first_needed = jnp.zeros((1, 128), jnp.int32)
    for pr in range(N_HEADS // 2):
        q2_t = (zb[:, pr * PAIR_W:(pr + 1) * PAIR_W] * QK_SCALE).T
        k2 = zb[:, GROUP_W + pr * PAIR_W:GROUP_W + (pr + 1) * PAIR_W]
        v2_t = zb[:, 2 * GROUP_W + pr * PAIR_W:2 * GROUP_W + (pr + 1) * PAIR_W].T
        for e in range(2):
            h = 2 * pr + e
            q_norm = jnp.sqrt(jnp.max(jnp.sum(jnp.square(head_rows(q2_t, e)), axis=0, keepdims=True),
                                      axis=1, keepdims=True)) * NORM_SLACK
            k_sq = jnp.where(lane // HEAD_DIM == e, jnp.square(k2), 0.0)
            k_norm = jnp.sqrt(jnp.max(jnp.sum(k_sq, axis=1, keepdims=True), axis=0, keepdims=True))
            hist_ref[h:h + 1, :] = jnp.where(tile_lane == i, k_norm, hist_ref[h:h + 1, :])
            hist_ref[N_HEADS + h:N_HEADS + h + 1, :] = jnp.where(
                tile_lane == i, cum[rows - 1:rows, h:h + 1], hist_ref[N_HEADS + h:N_HEADS + h + 1, :])
            upper = q_norm * hist_ref[h:h + 1, :] + cum[0:1, h:h + 1] - hist_ref[N_HEADS + h:N_HEADS + h + 1, :]
            skippable = (upper < -(q_norm * k_norm) - SKIP_LOG2) & (tile_lane < i)
            first_h = jnp.min(jnp.where(skippable, 128, tile_lane), axis=1, keepdims=True)
            first_needed = jnp.where(tile_lane == h, first_h, first_needed)
            c = jnp.broadcast_to(cum[:, h:h + 1], (rows, PAIR_W))
            hi = c.astype(BF16).astype(F32)
            mid = (c - hi).astype(BF16).astype(F32)
            lo = c - hi - mid
            k_aug = jnp.where(lane < HEAD_DIM, head_lanes_first(k2, e),
                    jnp.where(lane < HEAD_DIM + 3, 1.0,
                    jnp.where(lane == HEAD_DIM + 3, -hi,
                    jnp.where(lane == HEAD_DIM + 4, -mid,
                    jnp.where(lane == HEAD_DIM + 5, -lo, 0.0)))))
            c_t = jnp.broadcast_to(cum_t[h:h + 1, :], (8, rows))
            hi_t = c_t.astype(BF16).astype(F32)
            mid_t = (c_t - hi_t).astype(BF16).astype(F32)
            lo_t = c_t - hi_t - mid_t
            q_bias = jnp.where(sub8 == 0, hi_t, jnp.where(sub8 == 1, mid_t, jnp.where(sub8 == 2, lo_t,
                     jnp.where(sub8 < 6, 1.0, 0.0))))
            qb_ref[0, h] = jnp.concatenate(
                [head_rows(q2_t, e), q_bias, jnp.zeros((HEAD_PAD - HEAD_DIM - 8, rows), F32)],
                axis=0).astype(BF16)
            kb_ref[0, h] = k_aug.astype(BF16)
            vb_ref[0, h, 0] = jnp.concatenate([head_rows(v2_t, e), ones_slab], axis=0).astype(BF16)
    first_ref[0, 0] = jnp.broadcast_to(first_needed, (8, 128))


def _mixer_front(x, layer, w_in, w_rest, rot, bfg, lng, lnb, sgu_w, sgu_bias, avg, pool_w, pool_s):
    bsz, seq, _ = x.shape
    n_blk = seq // MOBA_BLOCK
    assert seq % ATTN_TK == 0 and ATTN_TK % FRONT_ROWS == 0
    assert n_blk <= HEAD_PAD - HEAD_DIM and n_blk % 8 == 0 and seq // FRONT_ROWS <= 128
    rows = FRONT_ROWS
    const = lambda *shape: pl.BlockSpec(shape, lambda b, i: (0,) * len(shape))
    of_layer = lambda *shape: pl.BlockSpec((None,) + shape, lambda b, i: (layer,) + (0,) * len(shape))
    per_tk = ATTN_TK // rows
    q_spec = pl.BlockSpec((1, N_HEADS, HEAD_PAD, rows), lambda b, i: (b, 0, 0, i))
    k_spec = pl.BlockSpec((1, N_HEADS, rows, HEAD_PAD), lambda b, i: (b, 0, i, 0))
    v_spec = pl.BlockSpec((1, N_HEADS, 1, V_ROWS, rows), lambda b, i: (b, 0, i // per_tk, 0, i % per_tk))
    flat_spec = pl.BlockSpec((1, rows, GROUP_W), lambda b, i: (b, i, 0))
    q_shape = jax.ShapeDtypeStruct((bsz, N_HEADS, HEAD_PAD, seq), BF16)
    k_shape = jax.ShapeDtypeStruct((bsz, N_HEADS, seq, HEAD_PAD), BF16)
    v_shape = jax.ShapeDtypeStruct((bsz, N_HEADS, seq // ATTN_TK, V_ROWS, ATTN_TK), BF16)
    flat_shape = jax.ShapeDtypeStruct((bsz, seq, GROUP_W), BF16)
    return pl.pallas_call(
        _front_kernel,
        grid=(bsz, seq // rows),
        in_specs=[
            pl.BlockSpec((1, rows, D_MODEL), lambda b, i: (b, i, 0)),
            pl.BlockSpec((None, D_MODEL, COL_FORGET), lambda b, i: (layer, 0, 0),
                         pipeline_mode=pl.Buffered(1)),
            of_layer(D_MODEL, REST_W),
            pl.BlockSpec((3, rows, PAIR_W), lambda b, i: (0, i, 0)),
            of_layer(8, 128), of_layer(1, GROUP_W), of_layer(1, GROUP_W),
            of_layer(SGU_GROUPS, SGU_CHUNK, SGU_CHUNK), of_layer(SGU_CHUNK, GROUP_W),
            const(GROUP_W, GROUP_W), of_layer(GROUP_W, GROUP_W), of_layer(1, GROUP_W),
        ],
        out_specs=[q_spec, k_spec, v_spec] * 2 + [flat_spec] * 2 + [
            pl.BlockSpec((1, 1, 8, 128), lambda b, i: (b, i, 0, 0))],
        out_shape=[q_shape, k_shape, v_shape] * 2 + [flat_shape] * 2 + [
            jax.ShapeDtypeStruct((bsz, seq // rows, 8, 128), jnp.int32)],
        scratch_shapes=[
            pltpu.VMEM((N_HEADS // 2, n_blk, PAIR_W), F32),
            pltpu.VMEM((8, 128), F32),
            pltpu.VMEM((POOL_HALO, GROUP_W), F32),
            pltpu.VMEM((2 * N_HEADS, 128), F32),
            pltpu.VMEM((D_MODEL, COL_FORGET), BF16),
        ],
        compiler_params=pltpu.CompilerParams(
            dimension_semantics=("arbitrary", "arbitrary"), vmem_limit_bytes=VMEM_LIMIT),
        name="mixer_front",
    )(x, w_in, w_rest, rot, bfg, lng, lnb, sgu_w, sgu_bias, avg, pool_w, pool_s)


def _attn_kernel(first_ref, qt_ref, k_ref, vt_ref, o_ref, acc_ref, sa_ref, sb_ref, mta_ref, mtb_ref):
    tq, tk = ATTN_TQ, ATTN_TK
    qi = pl.program_id(2)
    acc_ref[...] = jnp.zeros_like(acc_ref)
    n_sub = pl.num_programs(2) * (tq // tk)
    n_heads = pl.num_programs(1) * ATTN_HEADS
    first = 2 * qi
    for sub in range(tq // tk):
        for h in range(ATTN_HEADS):
            first = jnp.minimum(first, first_ref[((pl.program_id(0) * n_sub + 2 * qi + sub) * n_heads
                                                  + pl.program_id(1) * ATTN_HEADS + h)])
    first_pair = first // 2

    def produce(j, s_ref, mt_ref):
        off = pl.multiple_of(j * tk, tk)
        for h in range(ATTN_HEADS):
            s = jnp.dot(k_ref[0, h, pl.ds(off, tk), :], qt_ref[0, h], preferred_element_type=F32)
            s_ref[h] = s
            mt_ref[h, 0:1, :] = jnp.max(s, axis=0, keepdims=True)

    def consume(j, s_ref, mt_ref, ms, masked, lo=0):
        width = tq - lo
        new_ms = []
        for h in range(ATTN_HEADS):
            if masked:
                key_i = lax.broadcasted_iota(jnp.int32, (tk, width), 0) + (tk if lo else 0)
                qry_i = lax.broadcasted_iota(jnp.int32, (tk, width), 1) + lo
                s = jnp.where(key_i <= qry_i, s_ref[h, :, lo:], -jnp.inf)
                m_tile = jnp.max(s, axis=0, keepdims=True)
            else:
                s = s_ref[h]
                m_tile = mt_ref[h, 0:1, :]
            m_old = ms[h][:, lo:]
            m_new = jnp.maximum(m_old, m_tile)
            alpha = jnp.exp2(m_old - m_new)
            p = jnp.exp2(s - m_new).astype(BF16)
            acc_ref[h, :, lo:] = alpha * acc_ref[h, :, lo:] + jnp.dot(vt_ref[0, h, j], p, preferred_element_type=F32)
            new_ms.append(m_new)
        return tuple(new_ms)

    def produce_upper(j, s_ref):
        off = pl.multiple_of(j * tk, tk)
        for h in range(ATTN_HEADS):
            s_ref[h, :, tk:] = jnp.dot(k_ref[0, h, pl.ds(off, tk), :], qt_ref[0, h, :, tk:],
                                       preferred_element_type=F32)

    def pair(jj, ms):
        produce(2 * jj + 1, sb_ref, mtb_ref)
        ms = consume(2 * jj, sa_ref, mta_ref, ms, False)
        produce(2 * jj + 2, sa_ref, mta_ref)
        return consume(2 * jj + 1, sb_ref, mtb_ref, ms, False)

    produce(2 * first_pair, sa_ref, mta_ref)
    m0 = jnp.full((1, tq), -jnp.inf, F32)
    ms = lax.fori_loop(first_pair, qi, pair, (m0,) * ATTN_HEADS)
    produce_upper(2 * qi + 1, sb_ref)
    ms = consume(2 * qi, sa_ref, mta_ref, ms, True)
    consume(2 * qi + 1, sb_ref, mtb_ref, ms, True, lo=tk)

    outs = [(acc_ref[h] / acc_ref[h, HEAD_DIM:HEAD_DIM + 1, :])[0:HEAD_DIM] for h in range(ATTN_HEADS)]
    o_ref[0] = jnp.concatenate(outs, axis=0).T.astype(BF16)


def _causal_attn(qt, k, vt, first_tile):
    bsz, n_heads, seq, _ = k.shape
    assert ATTN_TQ == 2 * ATTN_TK and ATTN_TK == FRONT_ROWS and seq % ATTN_TQ == 0 and n_heads % ATTN_HEADS == 0
    grid_spec = pltpu.PrefetchScalarGridSpec(
        num_scalar_prefetch=1,
        grid=(bsz, n_heads // ATTN_HEADS, seq // ATTN_TQ),
        in_specs=[
            pl.BlockSpec((1, ATTN_HEADS, HEAD_PAD, ATTN_TQ), lambda b, hp, i, first: (b, hp, 0, i)),
            pl.BlockSpec((1, ATTN_HEADS, seq, HEAD_PAD), lambda b, hp, i, first: (b, hp, 0, 0)),
            pl.BlockSpec((1, ATTN_HEADS, seq // ATTN_TK, V_ROWS, ATTN_TK), lambda b, hp, i, first: (b, hp, 0, 0, 0)),
        ],
        out_specs=pl.BlockSpec((1, ATTN_TQ, ATTN_HEADS * HEAD_DIM), lambda b, hp, i, first: (b, i, hp)),
        scratch_shapes=[pltpu.VMEM((ATTN_HEADS, V_ROWS, ATTN_TQ), F32),
                        pltpu.VMEM((ATTN_HEADS, ATTN_TK, ATTN_TQ), F32),
                        pltpu.VMEM((ATTN_HEADS, ATTN_TK, ATTN_TQ), F32),
                        pltpu.VMEM((ATTN_HEADS, 8, ATTN_TQ), F32),
                        pltpu.VMEM((ATTN_HEADS, 8, ATTN_TQ), F32)],
    )
    return pl.pallas_call(
        _attn_kernel,
        grid_spec=grid_spec,
        out_shape=jax.ShapeDtypeStruct((bsz, seq, n_heads * HEAD_DIM), BF16),
        compiler_params=pltpu.CompilerParams(
            dimension_semantics=("parallel", "parallel", "arbitrary"), vmem_limit_bytes=VMEM_LIMIT),
        name="causal_attn",
    )(first_tile, qt, k, vt)


def _out_proj_kernel(ya_ref, yb_ref, yc_ref, yd_ref, x_ref, w_ref, g_ref, b_ref, o_ref):
    part = PROJ_ROWS // PROJ_PARTS
    spans = [slice(k * part, (k + 1) * part) for k in range(PROJ_PARTS)]

    def project(sp):
        y = jnp.concatenate([ya_ref[sp, :], yb_ref[sp, :], yc_ref[sp, :], yd_ref[sp, :]], axis=1)
        return jnp.dot(y, w_ref[...], preferred_element_type=F32)

    pending = project(spans[0])
    for k, sp in enumerate(spans):
        y = pending
        if k + 1 < PROJ_PARTS:
            pending = project(spans[k + 1])
        o_ref[sp, :] = _layer_norm_rows(DN_ALPHA * x_ref[sp, :] + y, g_ref[...], b_ref[...])


def _out_proj_ln(ya, yb, yc, yd, x, layer, w_o, g, b):
    m = x.shape[0]
    rows = PROJ_ROWS
    assert m % rows == 0
    part = pl.BlockSpec((rows, GROUP_W), lambda i: (i, 0))
    vec = pl.BlockSpec((None, 1, D_MODEL), lambda i: (layer, 0, 0))
    return pl.pallas_call(
        _out_proj_kernel,
        grid=(m // rows,),
        in_specs=[part, part, part, part,
                  pl.BlockSpec((rows, D_MODEL), lambda i: (i, 0)),
                  pl.BlockSpec((None, D_MODEL, D_MODEL), lambda i: (layer, 0, 0)), vec, vec],
        out_specs=pl.BlockSpec((rows, D_MODEL), lambda i: (i, 0)),
        out_shape=jax.ShapeDtypeStruct((m, D_MODEL), F32),
        compiler_params=pltpu.CompilerParams(dimension_semantics=("parallel",), vmem_limit_bytes=VMEM_LIMIT),
        name="out_proj_ln",
    )(ya, yb, yc, yd, x, w_o, g, b)


def _ffn_kernel(tiles_per_seq, x_ref, xp_ref, wu_ref, cw_ref, cb_ref, wd_ref, g_ref, b_ref, o_ref,
                xb_ref, ha_ref, hb_ref, acc_ref):
    m = pl.program_id(0)
    cols = FFN_COLS
    n_col = D_FF // cols
    prev = jnp.where(m % tiles_per_seq == 0, 0.0, xp_ref[...])
    xb_ref[0:FFN_HALO, :] = prev.astype(BF16)
    xb_ref[FFN_HALO:, :] = x_ref[...].astype(BF16)

    def both(n, ref):
        return jnp.concatenate([ref[:, n * cols:(n + 1) * cols], ref[:, D_FF + n * cols:D_FF + (n + 1) * cols]],
                               axis=1)

    def up(n, h_ref):
        xb = xb_ref[...]
        h_ref[:, 0:cols] = jnp.dot(xb, wu_ref[:, n * cols:(n + 1) * cols], preferred_element_type=F32)
        h_ref[:, cols:2 * cols] = jnp.dot(xb, wu_ref[:, D_FF + n * cols:D_FF + (n + 1) * cols],
                                          preferred_element_type=F32)

    def gated(n, h_ref):
        h = h_ref[...]
        cw = both(n, cw_ref)
        c = (pltpu.roll(h, 2, 0) * cw[0:1, :] + pltpu.roll(h, 1, 0) * cw[1:2, :] + h * cw[2:3, :]
             + both(n, cb_ref))[FFN_HALO:, :]
        gate, val = c[:, 0:cols], c[:, cols:2 * cols]
        return (gate * jax.nn.sigmoid(gate) * val).astype(BF16)

    def down(n, act):
        return jnp.dot(act, wd_ref[n * cols:(n + 1) * cols, :], preferred_element_type=F32)

    bufs = (ha_ref, hb_ref)
    up(0, bufs[0])
    parts = []
    for n in range(n_col):
        if n + 1 < n_col:
            up(n + 1, bufs[(n + 1) % 2])
        parts.append(down(n, gated(n, bufs[n % 2])))
        if len(parts) == 2:
            if n == 1:
                acc_ref[...] = parts[0] + parts[1]
            else:
                acc_ref[...] += parts[0] + parts[1]
            parts = []
    y = acc_ref[...] + parts[0]
    o_ref[...] = _layer_norm_rows(DN_ALPHA * x_ref[...] + y, g_ref[...], b_ref[...])


def _conv_ffn_ln(x, seq, layer, w_up, conv_w, conv_b, w_down, g, b):
    m = x.shape[0]
    rows = FFN_ROWS
    n_col = D_FF // FFN_COLS
    assert m % rows == 0 and seq % rows == 0 and rows % FFN_HALO == 0 and n_col % 2 == 1 and n_col >= 3
    halo_blocks = rows // FFN_HALO
    resident = lambda *shape: pl.BlockSpec((None,) + shape, lambda i: (layer,) + (0,) * len(shape),
                                           pipeline_mode=pl.Buffered(1))
    return pl.pallas_call(
        functools.partial(_ffn_kernel, seq // rows),
        grid=(m // rows,),
        in_specs=[
            pl.BlockSpec((rows, D_MODEL), lambda i: (i, 0)),
            pl.BlockSpec((FFN_HALO, D_MODEL), lambda i: (jnp.maximum(i * halo_blocks - 1, 0), 0)),
            resident(D_MODEL, 2 * D_FF),
            resident(3, 2 * D_FF),
            resident(1, 2 * D_FF),
            resident(D_FF, D_MODEL),
            resident(1, D_MODEL), resident(1, D_MODEL),
        ],
        out_specs=pl.BlockSpec((rows, D_MODEL), lambda i: (i, 0)),
        out_shape=jax.ShapeDtypeStruct((m, D_MODEL), F32),
        scratch_shapes=[pltpu.VMEM((FFN_HALO + rows, D_MODEL), BF16),
                        pltpu.VMEM((FFN_HALO + rows, 2 * FFN_COLS), F32),
                        pltpu.VMEM((FFN_HALO + rows, 2 * FFN_COLS), F32),
                        pltpu.VMEM((rows, D_MODEL), F32)],
        compiler_params=pltpu.CompilerParams(dimension_semantics=("parallel",), vmem_limit_bytes=VMEM_LIMIT),
        name="conv_ffn_ln",
    )(x, x, w_up, conv_w, conv_b, w_down, g, b)


def _rotary_lane_tables(seq):
    pos = jnp.arange(seq, dtype=F32)
    inv_freq = ROPE_THETA ** (-jnp.arange(0, ROPE_DIM, 2, dtype=F32) / ROPE_DIM)
    half = ROPE_DIM // 2
    d = np.arange(PAIR_W) % HEAD_DIM
    freq_lane = jnp.where(d < ROPE_DIM, inv_freq[d % half], 0.0)
    ang = pos[:, None] * freq_lane[None, :]
    cos, sin = jnp.cos(ang), jnp.sin(ang)
    sin_a = jnp.where(d < half, -sin, 0.0)
    sin_b = jnp.where((d >= half) & (d < ROPE_DIM), sin, 0.0)
    return jnp.stack([cos, sin_a, sin_b])


def kernel(x, w_in, b_forget, sgu_ln_g, sgu_ln_b, sgu_w, sgu_b, pool_w, pool_scale, w_o, ln1_g, ln1_b,
           w_up, conv_w, conv_b, w_down, ln2_g, ln2_b):
    bsz, seq, _ = x.shape
    depth = w_in.shape[0]
    row = lambda t: t.reshape(depth, 1, t.shape[-1])
    rot = _rotary_lane_tables(seq)
    group = jnp.arange(GROUP_W) // HEAD_DIM
    avg = jnp.where(group[:, None] == group[None, :], 1.0 / HEAD_DIM, 0.0).astype(BF16)
    f0 = COL_FORGET
    w_rest = jnp.concatenate([w_in[:, :, f0 + N_HEADS:], w_in[:, :, f0:f0 + N_HEADS],
                              jnp.zeros((depth, D_MODEL, 128 - N_HEADS), F32)], axis=-1).astype(BF16)
    bfg = jnp.broadcast_to(jnp.pad(b_forget, ((0, 0), (0, 8 - N_HEADS)))[:, :, None], (depth, 8, 128))
    sgu_bias = jnp.repeat(sgu_b.transpose(0, 2, 1), HEAD_DIM, axis=2)
    n_pool = len(POOL_WINDOWS)
    same_group = jnp.eye(n_pool, dtype=bool)[None, :, None, :, None]
    pool_bd = jnp.where(same_group, pool_w[:, :, :, None, :], 0.0).reshape(
        depth, GROUP_W, GROUP_W).astype(BF16)
    w_o_b, w_up_b, w_down_b = w_o.astype(BF16), w_up.astype(BF16), w_down.astype(BF16)

    xf = x.reshape(bsz * seq, D_MODEL)
    for l in range(depth):
        qa, ka, va, qb, kb, vb, yc, yd, first = _mixer_front(
            xf.reshape(bsz, seq, D_MODEL), l, w_in, w_rest, rot, bfg, row(sgu_ln_g), row(sgu_ln_b), sgu_w,
            sgu_bias, avg, pool_bd, row(pool_scale))
        first_b = first[:, :, 0, :N_HEADS].reshape(-1)
        ya = _causal_attn(qa, ka, va, jnp.zeros_like(first_b)).reshape(bsz * seq, GROUP_W)
        yb = _causal_attn(qb, kb, vb, first_b).reshape(bsz * seq, GROUP_W)
        xf = _out_proj_ln(ya, yb, yc.reshape(bsz * seq, GROUP_W), yd.reshape(bsz * seq, GROUP_W), xf, l,
                          w_o_b, row(ln1_g), row(ln1_b))
        xf = _conv_ffn_ln(xf, seq, l, w_up_b, conv_w, row(conv_b), w_down_b, row(ln2_g), row(ln2_b))
    return xf.reshape(bsz, seq, D_MODEL)
```

```python
import functools

import numpy as np
import jax
import jax.numpy as jnp
from jax import lax
from jax.experimental import pallas as pl
from jax.experimental.pallas import tpu as pltpu

F32 = jnp.float32
BF16 = jnp.bfloat16

D_MODEL = 1024
DEPTH = 2
HEAD_DIM = 64
N_HEADS = 4
GROUP_W = 256
HEAD_PAD = 128
PAIR_W = 2 * HEAD_DIM
V_ROWS = 80
MOBA_BLOCK = 256
MOBA_TOPK = 3
ROPE_THETA = 500000.0
ROPE_DIM = HEAD_DIM // 4
SGU_CHUNK = 128
SGU_GROUPS = 4
POOL_WINDOWS = (2, 4, 8, 16)
POOL_HALO = 16
D_FF = 2816
DN_ALPHA = (2 * DEPTH) ** 0.25
LN_EPS = 1e-5
LOG2_E = 1.4426950408889634
QK_SCALE = HEAD_DIM ** -0.5 * LOG2_E
MASK_BIAS = -1e30
SKIP_LOG2 = 160.0
NORM_SLACK = 1.02

FRONT_ROWS = MOBA_BLOCK
ATTN_TQ = 512
ATTN_TK = 256
ATTN_HEADS = 4
PROJ_ROWS = 1024
PROJ_PARTS = 4
FFN_ROWS = 512
FFN_COLS = 256
FFN_HALO = 16
VMEM_LIMIT = 56 * 1024 * 1024

COL_A = 0
COL_B = 3 * GROUP_W
COL_FORGET = 6 * GROUP_W
REST_W = 3 * GROUP_W + 128


def _layer_norm_rows(r, g, b):
    mu = jnp.mean(r, axis=-1, keepdims=True)
    d = r - mu
    var = jnp.mean(d * d, axis=-1, keepdims=True)
    return d * lax.rsqrt(var + LN_EPS) * g + b


def _split_bf16(t):
    hi = t.astype(BF16)
    lo = (t - hi.astype(F32)).astype(BF16)
    return hi, lo


def _group_mean(t, avg):
    hi, lo = _split_bf16(t)
    return (jnp.dot(hi, avg, preferred_element_type=F32) + jnp.dot(lo, avg, preferred_element_type=F32))


def _front_kernel(x_ref, wqkv32_ref, wrest_ref, rot_ref, bfg_ref, lng_ref, lnb_ref, sw_ref, sb_ref,
                  avg_ref, pw_ref, ps_ref,
                  qa_ref, ka_ref, va_ref, qb_ref, kb_ref, vb_ref, yc_ref, yd_ref, first_ref,
                  kbar_ref, carry_ref, halo_ref, hist_ref, wqkv_ref):
    rows = FRONT_ROWS
    i = pl.program_id(1)
    n_blk = kbar_ref.shape[1]

    @pl.when((i == 0) & (pl.program_id(0) == 0))
    def _():
        wqkv_ref[...] = wqkv32_ref[...].astype(BF16)

    @pl.when(i == 0)
    def _():
        kbar_ref[...] = jnp.zeros_like(kbar_ref)
        carry_ref[...] = jnp.zeros_like(carry_ref)
        halo_ref[...] = jnp.zeros_like(halo_ref)
        hist_ref[...] = jnp.zeros_like(hist_ref)

    xb = x_ref[0].astype(BF16)
    lane = lax.broadcasted_iota(jnp.int32, (rows, PAIR_W), 1)
    ones_slab = jnp.where(lax.broadcasted_iota(jnp.int32, (V_ROWS - HEAD_DIM, rows), 0) == 0, 1.0, 0.0)

    def head_rows(pair_t, e):
        return pair_t[e * HEAD_DIM:(e + 1) * HEAD_DIM]

    def head_lanes_first(pair, e):
        return pair if e == 0 else pltpu.roll(pair, HEAD_DIM, 1)

    cv = jnp.dot(xb, wrest_ref[:, GROUP_W:2 * GROUP_W], preferred_element_type=F32)
    cu = jnp.dot(xb, wrest_ref[:, 0:GROUP_W], preferred_element_type=F32)
    dp_fl = jnp.dot(xb, wrest_ref[:, 2 * GROUP_W:REST_W], preferred_element_type=F32)
    za = jnp.dot(xb, wqkv_ref[:, COL_A:COL_A + 3 * GROUP_W], preferred_element_type=F32)
    cos_t, sin_a, sin_b = rot_ref[0], rot_ref[1], rot_ref[2]

    def rotary(t):
        return (t * cos_t + pltpu.roll(t, PAIR_W - ROPE_DIM // 2, 1) * sin_a
                + pltpu.roll(t, ROPE_DIM // 2, 1) * sin_b)

    dp = dp_fl[:, 0:GROUP_W]

    inv_sqrt2 = np.float32(1.0 / np.sqrt(2.0))
    u = 0.5 * cu * (1.0 + lax.erf(cu * inv_sqrt2))
    vg = 0.5 * cv * (1.0 + lax.erf(cv * inv_sqrt2))
    avg = avg_ref[...]
    mu = _group_mean(vg, avg)

    sub8 = lax.broadcasted_iota(jnp.int32, (8, rows), 0)
    fl = dp_fl[:, GROUP_W:GROUP_W + 128].T[0:8, :] + jnp.broadcast_to(bfg_ref[:, 0:1], (8, rows))
    log_f = jnp.minimum(fl, 0.0) - jnp.log1p(jnp.exp(-jnp.abs(fl)))
    log_f = jnp.where(sub8 < N_HEADS, log_f, 0.0)
    r_i = lax.broadcasted_iota(jnp.int32, (rows, rows), 0)
    c_i = lax.broadcasted_iota(jnp.int32, (rows, rows), 1)
    tri = jnp.where(r_i <= c_i, 1.0, 0.0).astype(BF16)
    f_hi = log_f.astype(BF16)
    f_res = log_f - f_hi.astype(F32)
    f_mid = f_res.astype(BF16)
    f_lo = (f_res - f_mid.astype(F32)).astype(BF16)
    cum_t = (jnp.dot(f_hi, tri, preferred_element_type=F32) + jnp.dot(f_mid, tri, preferred_element_type=F32)
             + jnp.dot(f_lo, tri, preferred_element_type=F32) + jnp.broadcast_to(carry_ref[:, 0:1], (8, rows)))
    carry_ref[...] = jnp.broadcast_to(cum_t[:, rows - 1:rows], (8, 128))

    zb = jnp.dot(xb, wqkv_ref[:, COL_B:COL_B + 3 * GROUP_W], preferred_element_type=F32)

    dv = vg - mu
    var = _group_mean(dv * dv, avg)
    vn = dv * lax.rsqrt(var + LN_EPS) * lng_ref[...] + lnb_ref[...]
    lane_w = lax.broadcasted_iota(jnp.int32, (SGU_CHUNK, GROUP_W), 1)
    t_i = lax.broadcasted_iota(jnp.int32, (SGU_CHUNK, SGU_CHUNK), 0)
    s_i = lax.broadcasted_iota(jnp.int32, (SGU_CHUNK, SGU_CHUNK), 1)
    w_tril = [jnp.where(s_i <= t_i, sw_ref[g], 0.0).astype(BF16) for g in range(SGU_GROUPS)]
    for c in range(rows // SGU_CHUNK):
        vn_c = vn[c * SGU_CHUNK:(c + 1) * SGU_CHUNK, :].astype(BF16)
        mixed = sb_ref[...]
        for g in range(SGU_GROUPS):
            mg = jnp.dot(w_tril[g], vn_c, preferred_element_type=F32)
            mixed = mixed + jnp.where(lane_w // HEAD_DIM == g, mg, 0.0)
        yc_ref[0, c * SGU_CHUNK:(c + 1) * SGU_CHUNK, :] = (
            u[c * SGU_CHUNK:(c + 1) * SGU_CHUNK, :] * mixed).astype(BF16)

    ext = jnp.concatenate([halo_ref[...], dp], axis=0)
    s2 = ext + pltpu.roll(ext, 1, 0)
    s4 = s2 + pltpu.roll(s2, 2, 0)
    s8 = s4 + pltpu.roll(s4, 4, 0)
    s16 = s8 + pltpu.roll(s8, 8, 0)
    lane_g = lax.broadcasted_iota(jnp.int32, (rows, GROUP_W), 1) // (GROUP_W // len(POOL_WINDOWS))
    t_glob = lax.broadcasted_iota(jnp.int32, (rows, GROUP_W), 0) + i * rows
    win = jnp.where(lane_g == 0, POOL_WINDOWS[0], jnp.where(lane_g == 1, POOL_WINDOWS[1],
          jnp.where(lane_g == 2, POOL_WINDOWS[2], POOL_WINDOWS[3])))
    wsum = jnp.where(lane_g == 0, s2[POOL_HALO:], jnp.where(lane_g == 1, s4[POOL_HALO:],
           jnp.where(lane_g == 2, s8[POOL_HALO:], s16[POOL_HALO:])))
    count = jnp.minimum(t_glob + 1, win).astype(F32)
    pooled = wsum / count - dp
    yd = jnp.dot(pooled.astype(BF16), pw_ref[...], preferred_element_type=F32) * ps_ref[...]
    yd_ref[0] = yd.astype(BF16)
    halo_ref[...] = dp[rows - POOL_HALO:, :]

    blk_row = lax.broadcasted_iota(jnp.int32, (n_blk, rows), 0)
    blk_lane = lax.broadcasted_iota(jnp.int32, (n_blk, PAIR_W), 1)
    for pr in range(N_HEADS // 2):
        q2 = rotary(za[:, pr * PAIR_W:(pr + 1) * PAIR_W])
        k2 = rotary(za[:, GROUP_W + pr * PAIR_W:GROUP_W + (pr + 1) * PAIR_W])
        q2_t = (q2 * QK_SCALE).T
        v2_t = za[:, 2 * GROUP_W + pr * PAIR_W:2 * GROUP_W + (pr + 1) * PAIR_W].T
        for e in range(2):
            h = 2 * pr + e
            kbar = jnp.where(blk_lane // HEAD_DIM == e, kbar_ref[pr], 0.0)
            gate_t = lax.dot_general(kbar, q2, (((1,), (1,)), ((), ())),
                                     precision=lax.Precision.HIGHEST, preferred_element_type=F32)
            g = jnp.where(blk_row < i, gate_t, -jnp.inf)
            chosen = jnp.zeros((n_blk, rows), F32)
            for _ in range(MOBA_TOPK):
                top = jnp.max(g, axis=0, keepdims=True)
                first = jnp.min(jnp.where(g == top, blk_row, n_blk), axis=0, keepdims=True)
                pick = blk_row == first
                chosen = jnp.where(pick, 1.0, chosen)
                g = jnp.where(pick, -jnp.inf, g)
            keep = jnp.where(blk_row < i, chosen, jnp.where(blk_row == i, 1.0, 0.0))
            bias_t = jnp.where(keep > 0.0, 0.0, MASK_BIAS)
            qa_ref[0, h] = jnp.concatenate(
                [head_rows(q2_t, e), bias_t, jnp.zeros((HEAD_PAD - HEAD_DIM - n_blk, rows), F32)],
                axis=0).astype(BF16)
            ka_ref[0, h] = jnp.where(lane < HEAD_DIM, head_lanes_first(k2, e),
                                     jnp.where(lane == HEAD_DIM + i, 1.0, 0.0)).astype(BF16)
            va_ref[0, h, 0] = jnp.concatenate([head_rows(v2_t, e), ones_slab], axis=0).astype(BF16)
        kbar_ref[pr, pl.ds(i, 1), :] = jnp.sum(k2, axis=0, keepdims=True) * (1.0 / MOBA_BLOCK)

    cum_t = cum_t * LOG2_E
    cum = jnp.concatenate([cum_t, jnp.zeros((128 - 8, rows), F32)], axis=0).T
    tile_lane = lax.broadcasted_iota(jnp.int32, (1, 128), 1)
    first_needed = jnp.zeros((1, 128), jnp.int32)
    for pr in range(N_HEADS // 2):
        q2_t = (zb[:, pr * PAIR_W:(pr + 1) * PAIR_W] * QK_SCALE).T
        k2 = zb[:, GROUP_W + pr * PAIR_W:GROUP_W + (pr + 1) * PAIR_W]
        v2_t = zb[:, 2 * GROUP_W + pr * PAIR_W:2 * GROUP_W + (pr + 1) * PAIR_W].T
        for e in range(2):
            h = 2 * pr + e
            q_norm = jnp.sqrt(jnp.max(jnp.sum(jnp.square(head_rows(q2_t, e)), axis=0, keepdims=True),
                                      axis=1, keepdims=True)) * NORM_SLACK
            k_sq = jnp.where(lane // HEAD_DIM == e, jnp.square(k2), 0.0)
            k_norm = jnp.sqrt(jnp.max(jnp.sum(k_sq, axis=1, keepdims=True), axis=0, keepdims=True))
            hist_ref[h:h + 1, :] = jnp.where(tile_lane == i, k_norm, hist_ref[h:h + 1, :])
            hist_ref[N_HEADS + h:N_HEADS + h + 1, :] = jnp.where(
                tile_lane == i, cum[rows - 1:rows, h:h + 1], hist_ref[N_HEADS + h:N_HEADS + h + 1, :])
            upper = q_norm * hist_ref[h:h + 1, :] + cum[0:1, h:h + 1] - hist_ref[N_HEADS + h:N_HEADS + h + 1, :]
            skippable = (upper < -(q_norm * k_norm) - SKIP_LOG2) & (tile_lane < i)
            first_h = jnp.min(jnp.where(skippable, 128, tile_lane), axis=1, keepdims=True)
            first_needed = jnp.where(tile_lane == h, first_h, first_needed)
            c = jnp.broadcast_to(cum[:, h:h + 1], (rows, PAIR_W))
            hi = c.astype(BF16).astype(F32)
            mid = (c - hi).astype(BF16).astype(F32)
            lo = c - hi - mid
            k_aug = jnp.where(lane < HEAD_DIM, head_lanes_first(k2, e),
                    jnp.where(lane < HEAD_DIM + 3, 1.0,
                    jnp.where(lane == HEAD_DIM + 3, -hi,
                    jnp.where(lane == HEAD_DIM + 4, -mid,
                    jnp.where(lane == HEAD_DIM + 5, -lo, 0.0)))))
            c_t = jnp.broadcast_to(cum_t[h:h + 1, :], (8, rows))
            hi_t = c_t.astype(BF16).astype(F32)
            mid_t = (c_t - hi_t).astype(BF16).astype(F32)
            lo_t = c_t - hi_t - mid_t
            q_bias = jnp.where(sub8 == 0, hi_t, jnp.where(sub8 == 1, mid_t, jnp.where(sub8 == 2, lo_t,
                     jnp.where(sub8 < 6, 1.0, 0.0))))
            qb_ref[0, h] = jnp.concatenate(
                [head_rows(q2_t, e), q_bias, jnp.zeros((HEAD_PAD - HEAD_DIM - 8, rows), F32)],
                axis=0).astype(BF16)
            kb_ref[0, h] = k_aug.astype(BF16)
            vb_ref[0, h, 0] = jnp.concatenate([head_rows(v2_t, e), ones_slab], axis=0).astype(BF16)
    first_ref[0, 0] = jnp.broadcast_to(first_needed, (8, 128))


def _mixer_front(x, layer, w_in, w_rest, rot, bfg, lng, lnb, sgu_w, sgu_bias, avg, pool_w, pool_s):
    bsz, seq, _ = x.shape
    n_blk = seq // MOBA_BLOCK
    assert seq % ATTN_TK == 0 and ATTN_TK % FRONT_ROWS == 0
    assert n_blk <= HEAD_PAD - HEAD_DIM and n_blk % 8 == 0 and seq // FRONT_ROWS <= 128
    rows = FRONT_ROWS
    const = lambda *shape: pl.BlockSpec(shape, lambda b, i: (0,) * len(shape))
    of_layer = lambda *shape: pl.BlockSpec((None,) + shape, lambda b, i: (layer,) + (0,) * len(shape))
    per_tk = ATTN_TK // rows
    q_spec = pl.BlockSpec((1, N_HEADS, HEAD_PAD, rows), lambda b, i: (b, 0, 0, i))
    k_spec = pl.BlockSpec((1, N_HEADS, rows, HEAD_PAD), lambda b, i: (b, 0, i, 0))
    v_spec = pl.BlockSpec((1, N_HEADS, 1, V_ROWS, rows), lambda b, i: (b, 0, i // per_tk, 0, i % per_tk))
    flat_spec = pl.BlockSpec((1, rows, GROUP_W), lambda b, i: (b, i, 0))
    q_shape = jax.ShapeDtypeStruct((bsz, N_HEADS, HEAD_PAD, seq), BF16)
    k_shape = jax.ShapeDtypeStruct((bsz, N_HEADS, seq, HEAD_PAD), BF16)
    v_shape = jax.ShapeDtypeStruct((bsz, N_HEADS, seq // ATTN_TK, V_ROWS, ATTN_TK), BF16)
    flat_shape = jax.ShapeDtypeStruct((bsz, seq, GROUP_W), BF16)
    return pl.pallas_call(
        _front_kernel,
        grid=(bsz, seq // rows),
        in_specs=[
            pl.BlockSpec((1, rows, D_MODEL), lambda b, i: (b, i, 0)),
            pl.BlockSpec((None, D_MODEL, COL_FORGET), lambda b, i: (layer, 0, 0),
                         pipeline_mode=pl.Buffered(1)),
            of_layer(D_MODEL, REST_W),
            pl.BlockSpec((3, rows, PAIR_W), lambda b, i: (0, i, 0)),
            of_layer(8, 128), of_layer(1, GROUP_W), of_layer(1, GROUP_W),
            of_layer(SGU_GROUPS, SGU_CHUNK, SGU_CHUNK), of_layer(SGU_CHUNK, GROUP_W),
            const(GROUP_W, GROUP_W), of_layer(GROUP_W, GROUP_W), of_layer(1, GROUP_W),
        ],
        out_specs=[q_spec, k_spec, v_spec] * 2 + [flat_spec] * 2 + [
            pl.BlockSpec((1, 1, 8, 128), lambda b, i: (b, i, 0, 0))],
        out_shape=[q_shape, k_shape, v_shape] * 2 + [flat_shape] * 2 + [
            jax.ShapeDtypeStruct((bsz, seq // rows, 8, 128), jnp.int32)],
        scratch_shapes=[
            pltpu.VMEM((N_HEADS // 2, n_blk, PAIR_W), F32),
            pltpu.VMEM((8, 128), F32),
            pltpu.VMEM((POOL_HALO, GROUP_W), F32),
            pltpu.VMEM((2 * N_HEADS, 128), F32),
            pltpu.VMEM((D_MODEL, COL_FORGET), BF16),
        ],
        compiler_params=pltpu.CompilerParams(
            dimension_semantics=("arbitrary", "arbitrary"), vmem_limit_bytes=VMEM_LIMIT),
        name="mixer_front",
    )(x, w_in, w_rest, rot, bfg, lng, lnb, sgu_w, sgu_bias, avg, pool_w, pool_s)


def _attn_kernel(first_ref, qt_ref, k_ref, vt_ref, o_ref, acc_ref, sa_ref, sb_ref, mta_ref, mtb_ref):
    tq, tk = ATTN_TQ, ATTN_TK
    qi = pl.program_id(2)
    acc_ref[...] = jnp.zeros_like(acc_ref)
    n_sub = pl.num_programs(2) * (tq // tk)
    n_heads = pl.num_programs(1) * ATTN_HEADS
    first = 2 * qi
    for sub in range(tq // tk):
        for h in range(ATTN_HEADS):
            first = jnp.minimum(first, first_ref[((pl.program_id(0) * n_sub + 2 * qi + sub) * n_heads
                                                  + pl.program_id(1) * ATTN_HEADS + h)])
    first_pair = first // 2

    def produce(j, s_ref, mt_ref):
        off = pl.multiple_of(j * tk, tk)
        for h in range(ATTN_HEADS):
            s = jnp.dot(k_ref[0, h, pl.ds(off, tk), :], qt_ref[0, h], preferred_element_type=F32)
            s_ref[h] = s
            mt_ref[h, 0:1, :] = jnp.max(s, axis=0, keepdims=True)

    def consume(j, s_ref, mt_ref, ms, masked, lo=0):
        width = tq - lo
        new_ms = []
        for h in range(ATTN_HEADS):
            if masked:
                key_i = lax.broadcasted_iota(jnp.int32, (tk, width), 0) + (tk if lo else 0)
                qry_i = lax.broadcasted_iota(jnp.int32, (tk, width), 1) + lo
                s = jnp.where(key_i <= qry_i, s_ref[h, :, lo:], -jnp.inf)
                m_tile = jnp.max(s, axis=0, keepdims=True)
            else:
                s = s_ref[h]
                m_tile = mt_ref[h, 0:1, :]
            m_old = ms[h][:, lo:]
            m_new = jnp.maximum(m_old, m_tile)
            alpha = jnp.exp2(m_old - m_new)
            p = jnp.exp2(s - m_new).astype(BF16)
            acc_ref[h, :, lo:] = alpha * acc_ref[h, :, lo:] + jnp.dot(vt_ref[0, h, j], p, preferred_element_type=F32)
            new_ms.append(m_new)
        return tuple(new_ms)

    def produce_upper(j, s_ref):
        off = pl.multiple_of(j * tk, tk)
        for h in range(ATTN_HEADS):
            s_ref[h, :, tk:] = jnp.dot(k_ref[0, h, pl.ds(off, tk), :], qt_ref[0, h, :, tk:],
                                       preferred_element_type=F32)

    def pair(jj, ms):
        produce(2 * jj + 1, sb_ref, mtb_ref)
        ms = consume(2 * jj, sa_ref, mta_ref, ms, False)
        produce(2 * jj + 2, sa_ref, mta_ref)
        return consume(2 * jj + 1, sb_ref, mtb_ref, ms, False)

    produce(2 * first_pair, sa_ref, mta_ref)
    m0 = jnp.full((1, tq), -jnp.inf, F32)
    ms = lax.fori_loop(first_pair, qi, pair, (m0,) * ATTN_HEADS)
    produce_upper(2 * qi + 1, sb_ref)
    ms = consume(2 * qi, sa_ref, mta_ref, ms, True)
    consume(2 * qi + 1, sb_ref, mtb_ref, ms, True, lo=tk)

    outs = [acc_ref[h, 0:HEAD_DIM, :] / acc_ref[h, HEAD_DIM:HEAD_DIM + 1, :] for h in range(ATTN_HEADS)]
    o_ref[0] = jnp.concatenate(outs, axis=0).T.astype(BF16)


def _causal_attn(qt, k, vt, first_tile):
    bsz, n_heads, seq, _ = k.shape
    assert ATTN_TQ == 2 * ATTN_TK and ATTN_TK == FRONT_ROWS and seq % ATTN_TQ == 0 and n_heads % ATTN_HEADS == 0
    grid_spec = pltpu.PrefetchScalarGridSpec(
        num_scalar_prefetch=1,
        grid=(bsz, n_heads // ATTN_HEADS, seq // ATTN_TQ),
        in_specs=[
            pl.BlockSpec((1, ATTN_HEADS, HEAD_PAD, ATTN_TQ), lambda b, hp, i, first: (b, hp, 0, i)),
            pl.BlockSpec((1, ATTN_HEADS, seq, HEAD_PAD), lambda b, hp, i, first: (b, hp, 0, 0)),
            pl.BlockSpec((1, ATTN_HEADS, seq // ATTN_TK, V_ROWS, ATTN_TK), lambda b, hp, i, first: (b, hp, 0, 0, 0)),
        ],
        out_specs=pl.BlockSpec((1, ATTN_TQ, ATTN_HEADS * HEAD_DIM), lambda b, hp, i, first: (b, i, hp)),
        scratch_shapes=[pltpu.VMEM((ATTN_HEADS, V_ROWS, ATTN_TQ), F32),
                        pltpu.VMEM((ATTN_HEADS, ATTN_TK, ATTN_TQ), F32),
                        pltpu.VMEM((ATTN_HEADS, ATTN_TK, ATTN_TQ), F32),
                        pltpu.VMEM((ATTN_HEADS, 8, ATTN_TQ), F32),
                        pltpu.VMEM((ATTN_HEADS, 8, ATTN_TQ), F32)],
    )
    return pl.pallas_call(
        _attn_kernel,
        grid_spec=grid_spec,
        out_shape=jax.ShapeDtypeStruct((bsz, seq, n_heads * HEAD_DIM), BF16),
        compiler_params=pltpu.CompilerParams(
            dimension_semantics=("parallel", "parallel", "arbitrary"), vmem_limit_bytes=VMEM_LIMIT),
        name="causal_attn",
    )(first_tile, qt, k, vt)


def _out_proj_kernel(ya_ref, yb_ref, yc_ref, yd_ref, x_ref, w_ref, g_ref, b_ref, o_ref):
    part = PROJ_ROWS // PROJ_PARTS
    spans = [slice(k * part, (k + 1) * part) for k in range(PROJ_PARTS)]

    def project(sp):
        y = jnp.concatenate([ya_ref[sp, :], yb_ref[sp, :], yc_ref[sp, :], yd_ref[sp, :]], axis=1)
        return jnp.dot(y, w_ref[...], preferred_element_type=F32)

    pending = project(spans[0])
    for k, sp in enumerate(spans):
        y = pending
        if k + 1 < PROJ_PARTS:
            pending = project(spans[k + 1])
        o_ref[sp, :] = _layer_norm_rows(DN_ALPHA * x_ref[sp, :] + y, g_ref[...], b_ref[...])


def _out_proj_ln(ya, yb, yc, yd, x, layer, w_o, g, b):
    m = x.shape[0]
    rows = PROJ_ROWS
    assert m % rows == 0
    part = pl.BlockSpec((rows, GROUP_W), lambda i: (i, 0))
    vec = pl.BlockSpec((None, 1, D_MODEL), lambda i: (layer, 0, 0))
    return pl.pallas_call(
        _out_proj_kernel,
        grid=(m // rows,),
        in_specs=[part, part, part, part,
                  pl.BlockSpec((rows, D_MODEL), lambda i: (i, 0)),
                  pl.BlockSpec((None, D_MODEL, D_MODEL), lambda i: (layer, 0, 0)), vec, vec],
        out_specs=pl.BlockSpec((rows, D_MODEL), lambda i: (i, 0)),
        out_shape=jax.ShapeDtypeStruct((m, D_MODEL), F32),
        compiler_params=pltpu.CompilerParams(dimension_semantics=("parallel",), vmem_limit_bytes=VMEM_LIMIT),
        name="out_proj_ln",
    )(ya, yb, yc, yd, x, w_o, g, b)


def _ffn_kernel(tiles_per_seq, x_ref, xp_ref, wu_ref, cw_ref, cb_ref, wd_ref, g_ref, b_ref, o_ref,
                xb_ref, ha_ref, hb_ref, acc_ref):
    m = pl.program_id(0)
    cols = FFN_COLS
    n_col = D_FF // cols
    prev = jnp.where(m % tiles_per_seq == 0, 0.0, xp_ref[...])
    xb_ref[0:FFN_HALO, :] = prev.astype(BF16)
    xb_ref[FFN_HALO:, :] = x_ref[...].astype(BF16)

    def both(n, ref):
        return jnp.concatenate([ref[:, n * cols:(n + 1) * cols], ref[:, D_FF + n * cols:D_FF + (n + 1) * cols]],
                               axis=1)

    def up(n, h_ref):
        xb = xb_ref[...]
        h_ref[:, 0:cols] = jnp.dot(xb, wu_ref[:, n * cols:(n + 1) * cols], preferred_element_type=F32)
        h_ref[:, cols:2 * cols] = jnp.dot(xb, wu_ref[:, D_FF + n * cols:D_FF + (n + 1) * cols],
                                          preferred_element_type=F32)

    def gated(n, h_ref):
        h = h_ref[...]
        cw = both(n, cw_ref)
        c = (pltpu.roll(h, 2, 0) * cw[0:1, :] + pltpu.roll(h, 1, 0) * cw[1:2, :] + h * cw[2:3, :]
             + both(n, cb_ref))[FFN_HALO:, :]
        gate, val = c[:, 0:cols], c[:, cols:2 * cols]
        return (gate * jax.nn.sigmoid(gate) * val).astype(BF16)

    def down(n, act):
        return jnp.dot(act, wd_ref[n * cols:(n + 1) * cols, :], preferred_element_type=F32)

    bufs = (ha_ref, hb_ref)
    up(0, bufs[0])
    parts = []
    for n in range(n_col):
        if n + 1 < n_col:
            up(n + 1, bufs[(n + 1) % 2])
        parts.append(down(n, gated(n, bufs[n % 2])))
        if len(parts) == 2:
            if n == 1:
                acc_ref[...] = parts[0] + parts[1]
            else:
                acc_ref[...] += parts[0] + parts[1]
            parts = []
    y = acc_ref[...] + parts[0]
    o_ref[...] = _layer_norm_rows(DN_ALPHA * x_ref[...] + y, g_ref[...], b_ref[...])


def _conv_ffn_ln(x, seq, layer, w_up, conv_w, conv_b, w_down, g, b):
    m = x.shape[0]
    rows = FFN_ROWS
    n_col = D_FF // FFN_COLS
    assert m % rows == 0 and seq % rows == 0 and rows % FFN_HALO == 0 and n_col % 2 == 1 and n_col >= 3
    halo_blocks = rows // FFN_HALO
    resident = lambda *shape: pl.BlockSpec((None,) + shape, lambda i: (layer,) + (0,) * len(shape),
                                           pipeline_mode=pl.Buffered(1))
    return pl.pallas_call(
        functools.partial(_ffn_kernel, seq // rows),
        grid=(m // rows,),
        in_specs=[
            pl.BlockSpec((rows, D_MODEL), lambda i: (i, 0)),
            pl.BlockSpec((FFN_HALO, D_MODEL), lambda i: (jnp.maximum(i * halo_blocks - 1, 0), 0)),
            resident(D_MODEL, 2 * D_FF),
            resident(3, 2 * D_FF),
            resident(1, 2 * D_FF),
            resident(D_FF, D_MODEL),
            resident(1, D_MODEL), resident(1, D_MODEL),
        ],
        out_specs=pl.BlockSpec((rows, D_MODEL), lambda i: (i, 0)),
        out_shape=jax.ShapeDtypeStruct((m, D_MODEL), F32),
        scratch_shapes=[pltpu.VMEM((FFN_HALO + rows, D_MODEL), BF16),
                        pltpu.VMEM((FFN_HALO + rows, 2 * FFN_COLS), F32),
                        pltpu.VMEM((FFN_HALO + rows, 2 * FFN_COLS), F32),
                        pltpu.VMEM((rows, D_MODEL), F32)],
        compiler_params=pltpu.CompilerParams(dimension_semantics=("parallel",), vmem_limit_bytes=VMEM_LIMIT),
        name="conv_ffn_ln",
    )(x, x, w_up, conv_w, conv_b, w_down, g, b)


def _rotary_lane_tables(seq):
    pos = jnp.arange(seq, dtype=F32)
    inv_freq = ROPE_THETA ** (-jnp.arange(0, ROPE_DIM, 2, dtype=F32) / ROPE_DIM)
    half = ROPE_DIM // 2
    d = np.arange(PAIR_W) % HEAD_DIM
    freq_lane = jnp.where(d < ROPE_DIM, inv_freq[d % half], 0.0)
    ang = pos[:, None] * freq_lane[None, :]
    cos, sin = jnp.cos(ang), jnp.sin(ang)
    sin_a = jnp.where(d < half, -sin, 0.0)
    sin_b = jnp.where((d >= half) & (d < ROPE_DIM), sin, 0.0)
    return jnp.stack([cos, sin_a, sin_b])


def kernel(x, w_in, b_forget, sgu_ln_g, sgu_ln_b, sgu_w, sgu_b, pool_w, pool_scale, w_o, ln1_g, ln1_b,
           w_up, conv_w, conv_b, w_down, ln2_g, ln2_b):
    bsz, seq, _ = x.shape
    depth = w_in.shape[0]
    row = lambda t: t.reshape(depth, 1, t.shape[-1])
    rot = _rotary_lane_tables(seq)
    group = jnp.arange(GROUP_W) // HEAD_DIM
    avg = jnp.where(group[:, None] == group[None, :], 1.0 / HEAD_DIM, 0.0).astype(BF16)
    f0 = COL_FORGET
    w_rest = jnp.concatenate([w_in[:, :, f0 + N_HEADS:], w_in[:, :, f0:f0 + N_HEADS],
                              jnp.zeros((depth, D_MODEL, 128 - N_HEADS), F32)], axis=-1).astype(BF16)
    bfg = jnp.broadcast_to(jnp.pad(b_forget, ((0, 0), (0, 8 - N_HEADS)))[:, :, None], (depth, 8, 128))
    sgu_bias = jnp.repeat(sgu_b.transpose(0, 2, 1), HEAD_DIM, axis=2)
    n_pool = len(POOL_WINDOWS)
    same_group = jnp.eye(n_pool, dtype=bool)[None, :, None, :, None]
    pool_bd = jnp.where(same_group, pool_w[:, :, :, None, :], 0.0).reshape(
        depth, GROUP_W, GROUP_W).astype(BF16)
    w_o_b, w_up_b, w_down_b = w_o.astype(BF16), w_up.astype(BF16), w_down.astype(BF16)

    xf = x.reshape(bsz * seq, D_MODEL)
    for l in range(depth):
        qa, ka, va, qb, kb, vb, yc, yd, first = _mixer_front(
            xf.reshape(bsz, seq, D_MODEL), l, w_in, w_rest, rot, bfg, row(sgu_ln_g), row(sgu_ln_b), sgu_w,
            sgu_bias, avg, pool_bd, row(pool_scale))
        first_b = first[:, :, 0, :N_HEADS].reshape(-1)
        ya = _causal_attn(qa, ka, va, jnp.zeros_like(first_b)).reshape(bsz * seq, GROUP_W)
        yb = _causal_attn(qb, kb, vb, first_b).reshape(bsz * seq, GROUP_W)
        xf = _out_proj_ln(ya, yb, yc.reshape(bsz * seq, GROUP_W), yd.reshape(bsz * seq, GROUP_W), xf, l,
                          w_o_b, row(ln1_g), row(ln1_b))
        xf = _conv_ffn_ln(xf, seq, l, w_up_b, conv_w, row(conv_b), w_down_b, row(ln2_g), row(ln2_b))
    return xf.reshape(bsz, seq, D_MODEL)
```

```python
import functools

import numpy as np
import jax
import jax.numpy as jnp
from jax import lax
from jax.experimental import pallas as pl
from jax.experimental.pallas import tpu as pltpu

F32 = jnp.float32
BF16 = jnp.bfloat16

D_MODEL = 1024
DEPTH = 2
HEAD_DIM = 64
N_HEADS = 4
GROUP_W = 256
HEAD_PAD = 128
PAIR_W = 2 * HEAD_DIM
V_ROWS = 80
MOBA_BLOCK = 256
MOBA_TOPK = 3
ROPE_THETA = 500000.0
ROPE_DIM = HEAD_DIM // 4
SGU_CHUNK = 128
SGU_GROUPS = 4
POOL_WINDOWS = (2, 4, 8, 16)
POOL_HALO = 16
D_FF = 2816
DN_ALPHA = (2 * DEPTH) ** 0.25
LN_EPS = 1e-5
LOG2_E = 1.4426950408889634
QK_SCALE = HEAD_DIM ** -0.5 * LOG2_E
MASK_BIAS = -1e30
SKIP_LOG2 = 160.0
NORM_SLACK = 1.02

FRONT_ROWS = MOBA_BLOCK
ATTN_TQ = 512
ATTN_TK = 256
ATTN_HEADS = 4
PROJ_ROWS = 1024
PROJ_PARTS = 4
FFN_ROWS = 512
FFN_COLS = 256
FFN_HALO = 16
VMEM_LIMIT = 56 * 1024 * 1024

COL_A = 0
COL_B = 3 * GROUP_W
COL_FORGET = 6 * GROUP_W
REST_W = 3 * GROUP_W + 128


def _layer_norm_rows(r, g, b):
    mu = jnp.mean(r, axis=-1, keepdims=True)
    d = r - mu
    var = jnp.mean(d * d, axis=-1, keepdims=True)
    return d * lax.rsqrt(var + LN_EPS) * g + b


def _split_bf16(t):
    hi = t.astype(BF16)
    lo = (t - hi.astype(F32)).astype(BF16)
    return hi, lo


def _group_mean(t, avg):
    hi, lo = _split_bf16(t)
    return (jnp.dot(hi, avg, preferred_element_type=F32) + jnp.dot(lo, avg, preferred_element_type=F32))


def _front_kernel(x_ref, wqkv32_ref, wrest_ref, rot_ref, bfg_ref, lng_ref, lnb_ref, sw_ref, sb_ref,
                  avg_ref, pw_ref, ps_ref,
                  qa_ref, ka_ref, va_ref, qb_ref, kb_ref, vb_ref, yc_ref, yd_ref, first_ref,
                  kbar_ref, carry_ref, halo_ref, hist_ref, wqkv_ref):
    rows = FRONT_ROWS
    i = pl.program_id(1)
    n_blk = kbar_ref.shape[1]

    @pl.when((i == 0) & (pl.program_id(0) == 0))
    def _():
        wqkv_ref[...] = wqkv32_ref[...].astype(BF16)

    @pl.when(i == 0)
    def _():
        kbar_ref[...] = jnp.zeros_like(kbar_ref)
        carry_ref[...] = jnp.zeros_like(carry_ref)
        halo_ref[...] = jnp.zeros_like(halo_ref)
        hist_ref[...] = jnp.zeros_like(hist_ref)

    xb = x_ref[0].astype(BF16)
    lane = lax.broadcasted_iota(jnp.int32, (rows, PAIR_W), 1)
    ones_slab = jnp.where(lax.broadcasted_iota(jnp.int32, (V_ROWS - HEAD_DIM, rows), 0) == 0, 1.0, 0.0)

    def head_rows(pair_t, e):
        return pair_t[e * HEAD_DIM:(e + 1) * HEAD_DIM]

    def head_lanes_first(pair, e):
        return pair if e == 0 else pltpu.roll(pair, HEAD_DIM, 1)

    cv = jnp.dot(xb, wrest_ref[:, GROUP_W:2 * GROUP_W], preferred_element_type=F32)
    cu = jnp.dot(xb, wrest_ref[:, 0:GROUP_W], preferred_element_type=F32)
    dp_fl = jnp.dot(xb, wrest_ref[:, 2 * GROUP_W:REST_W], preferred_element_type=F32)
    za = jnp.dot(xb, wqkv_ref[:, COL_A:COL_A + 3 * GROUP_W], preferred_element_type=F32)
    cos_t, sin_a, sin_b = rot_ref[0], rot_ref[1], rot_ref[2]

    def rotary(t):
        return (t * cos_t + pltpu.roll(t, PAIR_W - ROPE_DIM // 2, 1) * sin_a
                + pltpu.roll(t, ROPE_DIM // 2, 1) * sin_b)

    dp = dp_fl[:, 0:GROUP_W]

    inv_sqrt2 = np.float32(1.0 / np.sqrt(2.0))
    u = 0.5 * cu * (1.0 + lax.erf(cu * inv_sqrt2))
    vg = 0.5 * cv * (1.0 + lax.erf(cv * inv_sqrt2))
    avg = avg_ref[...]
    mu = _group_mean(vg, avg)

    sub8 = lax.broadcasted_iota(jnp.int32, (8, rows), 0)
    fl = dp_fl[:, GROUP_W:GROUP_W + 128].T[0:8, :] + jnp.broadcast_to(bfg_ref[:, 0:1], (8, rows))
    log_f = jnp.minimum(fl, 0.0) - jnp.log1p(jnp.exp(-jnp.abs(fl)))
    log_f = jnp.where(sub8 < N_HEADS, log_f, 0.0)
    r_i = lax.broadcasted_iota(jnp.int32, (rows, rows), 0)
    c_i = lax.broadcasted_iota(jnp.int32, (rows, rows), 1)
    tri = jnp.where(r_i <= c_i, 1.0, 0.0).astype(BF16)
    f_hi = log_f.astype(BF16)
    f_res = log_f - f_hi.astype(F32)
    f_mid = f_res.astype(BF16)
    f_lo = (f_res - f_mid.astype(F32)).astype(BF16)
    cum_t = (jnp.dot(f_hi, tri, preferred_element_type=F32) + jnp.dot(f_mid, tri, preferred_element_type=F32)
             + jnp.dot(f_lo, tri, preferred_element_type=F32) + jnp.broadcast_to(carry_ref[:, 0:1], (8, rows)))
    carry_ref[...] = jnp.broadcast_to(cum_t[:, rows - 1:rows], (8, 128))

    zb = jnp.dot(xb, wqkv_ref[:, COL_B:COL_B + 3 * GROUP_W], preferred_element_type=F32)

    dv = vg - mu
    var = _group_mean(dv * dv, avg)
    vn = dv * lax.rsqrt(var + LN_EPS) * lng_ref[...] + lnb_ref[...]
    lane_w = lax.broadcasted_iota(jnp.int32, (SGU_CHUNK, GROUP_W), 1)
    t_i = lax.broadcasted_iota(jnp.int32, (SGU_CHUNK, SGU_CHUNK), 0)
    s_i = lax.broadcasted_iota(jnp.int32, (SGU_CHUNK, SGU_CHUNK), 1)
    w_tril = [jnp.where(s_i <= t_i, sw_ref[g], 0.0).astype(BF16) for g in range(SGU_GROUPS)]
    for c in range(rows // SGU_CHUNK):
        vn_c = vn[c * SGU_CHUNK:(c + 1) * SGU_CHUNK, :].astype(BF16)
        mixed = sb_ref[...]
        for g in range(SGU_GROUPS):
            mg = jnp.dot(w_tril[g], vn_c, preferred_element_type=F32)
            mixed = mixed + jnp.where(lane_w // HEAD_DIM == g, mg, 0.0)
        yc_ref[0, c * SGU_CHUNK:(c + 1) * SGU_CHUNK, :] = (
            u[c * SGU_CHUNK:(c + 1) * SGU_CHUNK, :] * mixed).astype(BF16)

    ext = jnp.concatenate([halo_ref[...], dp], axis=0)
    s2 = ext + pltpu.roll(ext, 1, 0)
    s4 = s2 + pltpu.roll(s2, 2, 0)
    s8 = s4 + pltpu.roll(s4, 4, 0)
    s16 = s8 + pltpu.roll(s8, 8, 0)
    lane_g = lax.broadcasted_iota(jnp.int32, (rows, GROUP_W), 1) // (GROUP_W // len(POOL_WINDOWS))
    t_glob = lax.broadcasted_iota(jnp.int32, (rows, GROUP_W), 0) + i * rows
    win = jnp.where(lane_g == 0, POOL_WINDOWS[0], jnp.where(lane_g == 1, POOL_WINDOWS[1],
          jnp.where(lane_g == 2, POOL_WINDOWS[2], POOL_WINDOWS[3])))
    wsum = jnp.where(lane_g == 0, s2[POOL_HALO:], jnp.where(lane_g == 1, s4[POOL_HALO:],
           jnp.where(lane_g == 2, s8[POOL_HALO:], s16[POOL_HALO:])))
    count = jnp.minimum(t_glob + 1, win).astype(F32)
    pooled = wsum / count - dp
    yd = jnp.dot(pooled.astype(BF16), pw_ref[...], preferred_element_type=F32) * ps_ref[...]
    yd_ref[0] = yd.astype(BF16)
    halo_ref[...] = dp[rows - POOL_HALO:, :]

    blk_row = lax.broadcasted_iota(jnp.int32, (n_blk, rows), 0)
    blk_lane = lax.broadcasted_iota(jnp.int32, (n_blk, PAIR_W), 1)
    for pr in range(N_HEADS // 2):
        q2 = rotary(za[:, pr * PAIR_W:(pr + 1) * PAIR_W])
        k2 = rotary(za[:, GROUP_W + pr * PAIR_W:GROUP_W + (pr + 1) * PAIR_W])
        q2_t = (q2 * QK_SCALE).T
        v2_t = za[:, 2 * GROUP_W + pr * PAIR_W:2 * GROUP_W + (pr + 1) * PAIR_W].T
        for e in range(2):
            h = 2 * pr + e
            kbar = jnp.where(blk_lane // HEAD_DIM == e, kbar_ref[pr], 0.0)
            gate_t = lax.dot_general(kbar, q2, (((1,), (1,)), ((), ())),
                                     precision=lax.Precision.HIGHEST, preferred_element_type=F32)
            g = jnp.where(blk_row < i, gate_t, -jnp.inf)
            chosen = jnp.zeros((n_blk, rows), F32)
            for _ in range(MOBA_TOPK):
                top = jnp.max(g, axis=0, keepdims=True)
                first = jnp.min(jnp.where(g == top, blk_row, n_blk), axis=0, keepdims=True)
                pick = blk_row == first
                chosen = jnp.where(pick, 1.0, chosen)
                g = jnp.where(pick, -jnp.inf, g)
            keep = jnp.where(blk_row < i, chosen, jnp.where(blk_row == i, 1.0, 0.0))
            bias_t = jnp.where(keep > 0.0, 0.0, MASK_BIAS)
            qa_ref[0, h] = jnp.concatenate(
                [head_rows(q2_t, e), bias_t, jnp.zeros((HEAD_PAD - HEAD_DIM - n_blk, rows), F32)],
                axis=0).astype(BF16)
            ka_ref[0, h] = jnp.where(lane < HEAD_DIM, head_lanes_first(k2, e),
                                     jnp.where(lane == HEAD_DIM + i, 1.0, 0.0)).astype(BF16)
            va_ref[0, h, 0] = jnp.concatenate([head_rows(v2_t, e), ones_slab], axis=0).astype(BF16)
        kbar_ref[pr, pl.ds(i, 1), :] = jnp.sum(k2, axis=0, keepdims=True) * (1.0 / MOBA_BLOCK)

    cum_t = cum_t * LOG2_E
    cum = jnp.concatenate([cum_t, jnp.zeros((128 - 8, rows), F32)], axis=0).T
    tile_lane = lax.broadcasted_iota(jnp.int32, (1, 128), 1)
    first_needed = jnp.zeros((1, 128), jnp.int32)
    for pr in range(N_HEADS // 2):
        q2_t = (zb[:, pr * PAIR_W:(pr + 1) * PAIR_W] * QK_SCALE).T
        k2 = zb[:, GROUP_W + pr * PAIR_W:GROUP_W + (pr + 1) * PAIR_W]
        v2_t = zb[:, 2 * GROUP_W + pr * PAIR_W:2 * GROUP_W + (pr + 1) * PAIR_W].T
        for e in range(2):
            h = 2 * pr + e
            q_norm = jnp.sqrt(jnp.max(jnp.sum(jnp.square(head_rows(q2_t, e)), axis=0, keepdims=True),
                                      axis=1, keepdims=True)) * NORM_SLACK
            k_sq = jnp.where(lane // HEAD_DIM == e, jnp.square(k2), 0.0)
            k_norm = jnp.sqrt(jnp.max(jnp.sum(k_sq, axis=1, keepdims=True), axis=0, keepdims=True))
            hist_ref[h:h + 1, :] = jnp.where(tile_lane == i, k_norm, hist_ref[h:h + 1, :])
            hist_ref[N_HEADS + h:N_HEADS + h + 1, :] = jnp.where(
                tile_lane == i, cum[rows - 1:rows, h:h + 1], hist_ref[N_HEADS + h:N_HEADS + h + 1, :])
            upper = q_norm * hist_ref[h:h + 1, :] + cum[0:1, h:h + 1] - hist_ref[N_HEADS + h:N_HEADS + h + 1, :]
            skippable = (upper < -(q_norm * k_norm) - SKIP_LOG2) & (tile_lane < i)
            first_h = jnp.min(jnp.where(skippable, 128, tile_lane), axis=1, keepdims=True)
            first_needed = jnp.where(tile_lane == h, first_h, first_needed)
            c = jnp.broadcast_to(cum[:, h:h + 1], (rows, PAIR_W))
            hi = c.astype(BF16).astype(F32)
            mid = (c - hi).astype(BF16).astype(F32)
            lo = c - hi - mid
            k_aug = jnp.where(lane < HEAD_DIM, head_lanes_first(k2, e),
                    jnp.where(lane < HEAD_DIM + 3, 1.0,
                    jnp.where(lane == HEAD_DIM + 3, -hi,
                    jnp.where(lane == HEAD_DIM + 4, -mid,
                    jnp.where(lane == HEAD_DIM + 5, -lo, 0.0)))))
            c_t = jnp.broadcast_to(cum_t[h:h + 1, :], (8, rows))
            hi_t = c_t.astype(BF16).astype(F32)
            mid_t = (c_t - hi_t).astype(BF16).astype(F32)
            lo_t = c_t - hi_t - mid_t
            q_bias = jnp.where(sub8 == 0, hi_t, jnp.where(sub8 == 1, mid_t, jnp.where(sub8 == 2, lo_t,
                     jnp.where(sub8 < 6, 1.0, 0.0))))
            qb_ref[0, h] = jnp.concatenate(
                [head_rows(q2_t, e), q_bias, jnp.zeros((HEAD_PAD - HEAD_DIM - 8, rows), F32)],
                axis=0).astype(BF16)
            kb_ref[0, h] = k_aug.astype(BF16)
            vb_ref[0, h, 0] = jnp.concatenate([head_rows(v2_t, e), ones_slab], axis=0).astype(BF16)
    first_ref[0, 0] = jnp.broadcast_to(first_needed, (8, 128))


def _mixer_front(x, layer, w_in, w_rest, rot, bfg, lng, lnb, sgu_w, sgu_bias, avg, pool_w, pool_s):
    bsz, seq, _ = x.shape
    n_blk = seq // MOBA_BLOCK
    assert seq % ATTN_TK == 0 and ATTN_TK % FRONT_ROWS == 0
    assert n_blk <= HEAD_PAD - HEAD_DIM and n_blk % 8 == 0 and seq // FRONT_ROWS <= 128
    rows = FRONT_ROWS
    const = lambda *shape: pl.BlockSpec(shape, lambda b, i: (0,) * len(shape))
    of_layer = lambda *shape: pl.BlockSpec((None,) + shape, lambda b, i: (layer,) + (0,) * len(shape))
    per_tk = ATTN_TK // rows
    q_spec = pl.BlockSpec((1, N_HEADS, HEAD_PAD, rows), lambda b, i: (b, 0, 0, i))
    k_spec = pl.BlockSpec((1, N_HEADS, rows, HEAD_PAD), lambda b, i: (b, 0, i, 0))
    v_spec = pl.BlockSpec((1, N_HEADS, 1, V_ROWS, rows), lambda b, i: (b, 0, i // per_tk, 0, i % per_tk))
    flat_spec = pl.BlockSpec((1, rows, GROUP_W), lambda b, i: (b, i, 0))
    q_shape = jax.ShapeDtypeStruct((bsz, N_HEADS, HEAD_PAD, seq), BF16)
    k_shape = jax.ShapeDtypeStruct((bsz, N_HEADS, seq, HEAD_PAD), BF16)
    v_shape = jax.ShapeDtypeStruct((bsz, N_HEADS, seq // ATTN_TK, V_ROWS, ATTN_TK), BF16)
    flat_shape = jax.ShapeDtypeStruct((bsz, seq, GROUP_W), BF16)
    return pl.pallas_call(
        _front_kernel,
        grid=(bsz, seq // rows),
        in_specs=[
            pl.BlockSpec((1, rows, D_MODEL), lambda b, i: (b, i, 0)),
            pl.BlockSpec((None, D_MODEL, COL_FORGET), lambda b, i: (layer, 0, 0),
                         pipeline_mode=pl.Buffered(1)),
            of_layer(D_MODEL, REST_W),
            pl.BlockSpec((3, rows, PAIR_W), lambda b, i: (0, i, 0)),
            of_layer(8, 128), of_layer(1, GROUP_W), of_layer(1, GROUP_W),
            of_layer(SGU_GROUPS, SGU_CHUNK, SGU_CHUNK), of_layer(SGU_CHUNK, GROUP_W),
            const(GROUP_W, GROUP_W), of_layer(GROUP_W, GROUP_W), of_layer(1, GROUP_W),
        ],
        out_specs=[q_spec, k_spec, v_spec] * 2 + [flat_spec] * 2 + [
            pl.BlockSpec((1, 1, 8, 128), lambda b, i: (b, i, 0, 0))],
        out_shape=[q_shape, k_shape, v_shape] * 2 + [flat_shape] * 2 + [
            jax.ShapeDtypeStruct((bsz, seq // rows, 8, 128), jnp.int32)],
        scratch_shapes=[
            pltpu.VMEM((N_HEADS // 2, n_blk, PAIR_W), F32),
            pltpu.VMEM((8, 128), F32),
            pltpu.VMEM((POOL_HALO, GROUP_W), F32),
            pltpu.VMEM((2 * N_HEADS, 128), F32),
            pltpu.VMEM((D_MODEL, COL_FORGET), BF16),
        ],
        compiler_params=pltpu.CompilerParams(
            dimension_semantics=("arbitrary", "arbitrary"), vmem_limit_bytes=VMEM_LIMIT),
        name="mixer_front",
    )(x, w_in, w_rest, rot, bfg, lng, lnb, sgu_w, sgu_bias, avg, pool_w, pool_s)


def _attn_kernel(first_ref, qt_ref, k_ref, vt_ref, o_ref, acc_ref, sa_ref, sb_ref, mta_ref, mtb_ref):
    tq, tk = ATTN_TQ, ATTN_TK
    qi = pl.program_id(2)
    acc_ref[...] = jnp.zeros_like(acc_ref)
    n_sub = pl.num_programs(2) * (tq // tk)
    n_heads = pl.num_programs(1) * ATTN_HEADS
    first = 2 * qi
    for sub in range(tq // tk):
        for h in range(ATTN_HEADS):
            first = jnp.minimum(first, first_ref[((pl.program_id(0) * n_sub + 2 * qi + sub) * n_heads
                                                  + pl.program_id(1) * ATTN_HEADS + h)])
    first_pair = first // 2

    def produce(j, s_ref, mt_ref):
        off = pl.multiple_of(j * tk, tk)
        for h in range(ATTN_HEADS):
            s = jnp.dot(k_ref[0, h, pl.ds(off, tk), :], qt_ref[0, h], preferred_element_type=F32)
            s_ref[h] = s
            mt_ref[h, 0:1, :] = jnp.max(s, axis=0, keepdims=True)

    def consume(j, s_ref, mt_ref, ms, masked, lo=0):
        width = tq - lo
        new_ms = []
        for h in range(ATTN_HEADS):
            if masked:
                tri = (lax.broadcasted_iota(jnp.int32, (tk, tk), 0) <= lax.broadcasted_iota(jnp.int32, (tk, tk), 1))
                s = jnp.where(tri, s_ref[h, :, lo:lo + tk], -jnp.inf)
                if width > tk:
                    s = jnp.concatenate([s, s_ref[h, :, lo + tk:]], axis=1)
                m_tile = jnp.max(s, axis=0, keepdims=True)
            else:
                s = s_ref[h]
                m_tile = mt_ref[h, 0:1, :]
            m_old = ms[h][:, lo:]
            m_new = jnp.maximum(m_old, m_tile)
            alpha = jnp.exp2(m_old - m_new)
            p = jnp.exp2(s - m_new).astype(BF16)
            acc_ref[h, :, lo:] = alpha * acc_ref[h, :, lo:] + jnp.dot(vt_ref[0, h, j], p, preferred_element_type=F32)
            new_ms.append(m_new)
        return tuple(new_ms)

    def produce_upper(j, s_ref):
        off = pl.multiple_of(j * tk, tk)
        for h in range(ATTN_HEADS):
            s_ref[h, :, tk:] = jnp.dot(k_ref[0, h, pl.ds(off, tk), :], qt_ref[0, h, :, tk:],
                                       preferred_element_type=F32)

    def pair(jj, ms):
        produce(2 * jj + 1, sb_ref, mtb_ref)
        ms = consume(2 * jj, sa_ref, mta_ref, ms, False)
        produce(2 * jj + 2, sa_ref, mta_ref)
        return consume(2 * jj + 1, sb_ref, mtb_ref, ms, False)

    produce(2 * first_pair, sa_ref, mta_ref)
    m0 = jnp.full((1, tq), -jnp.inf, F32)
    ms = lax.fori_loop(first_pair, qi, pair, (m0,) * ATTN_HEADS)
    produce_upper(2 * qi + 1, sb_ref)
    ms = consume(2 * qi, sa_ref, mta_ref, ms, True)
    consume(2 * qi + 1, sb_ref, mtb_ref, ms, True, lo=tk)

    outs = [acc_ref[h, 0:HEAD_DIM, :] / acc_ref[h, HEAD_DIM:HEAD_DIM + 1, :] for h in range(ATTN_HEADS)]
    o_ref[0] = jnp.concatenate(outs, axis=0).T.astype(BF16)


def _causal_attn(qt, k, vt, first_tile):
    bsz, n_heads, seq, _ = k.shape
    assert ATTN_TQ == 2 * ATTN_TK and ATTN_TK == FRONT_ROWS and seq % ATTN_TQ == 0 and n_heads % ATTN_HEADS == 0
    grid_spec = pltpu.PrefetchScalarGridSpec(
        num_scalar_prefetch=1,
        grid=(bsz, n_heads // ATTN_HEADS, seq // ATTN_TQ),
        in_specs=[
            pl.BlockSpec((1, ATTN_HEADS, HEAD_PAD, ATTN_TQ), lambda b, hp, i, first: (b, hp, 0, i)),
            pl.BlockSpec((1, ATTN_HEADS, seq, HEAD_PAD), lambda b, hp, i, first: (b, hp, 0, 0)),
            pl.BlockSpec((1, ATTN_HEADS, seq // ATTN_TK, V_ROWS, ATTN_TK), lambda b, hp, i, first: (b, hp, 0, 0, 0)),
        ],
        out_specs=pl.BlockSpec((1, ATTN_TQ, ATTN_HEADS * HEAD_DIM), lambda b, hp, i, first: (b, i, hp)),
        scratch_shapes=[pltpu.VMEM((ATTN_HEADS, V_ROWS, ATTN_TQ), F32),
                        pltpu.VMEM((ATTN_HEADS, ATTN_TK, ATTN_TQ), F32),
                        pltpu.VMEM((ATTN_HEADS, ATTN_TK, ATTN_TQ), F32),
                        pltpu.VMEM((ATTN_HEADS, 8, ATTN_TQ), F32),
                        pltpu.VMEM((ATTN_HEADS, 8, ATTN_TQ), F32)],
    )
    return pl.pallas_call(
        _attn_kernel,
        grid_spec=grid_spec,
        out_shape=jax.ShapeDtypeStruct((bsz, seq, n_heads * HEAD_DIM), BF16),
        compiler_params=pltpu.CompilerParams(
            dimension_semantics=("parallel", "parallel", "arbitrary"), vmem_limit_bytes=VMEM_LIMIT),
        name="causal_attn",
    )(first_tile, qt, k, vt)


def _out_proj_kernel(ya_ref, yb_ref, yc_ref, yd_ref, x_ref, w_ref, g_ref, b_ref, o_ref):
    part = PROJ_ROWS // PROJ_PARTS
    spans = [slice(k * part, (k + 1) * part) for k in range(PROJ_PARTS)]

    def project(sp):
        y = jnp.concatenate([ya_ref[sp, :], yb_ref[sp, :], yc_ref[sp, :], yd_ref[sp, :]], axis=1)
        return jnp.dot(y, w_ref[...], preferred_element_type=F32)

    pending = project(spans[0])
    for k, sp in enumerate(spans):
        y = pending
        if k + 1 < PROJ_PARTS:
            pending = project(spans[k + 1])
        o_ref[sp, :] = _layer_norm_rows(DN_ALPHA * x_ref[sp, :] + y, g_ref[...], b_ref[...])


def _out_proj_ln(ya, yb, yc, yd, x, layer, w_o, g, b):
    m = x.shape[0]
    rows = PROJ_ROWS
    assert m % rows == 0
    part = pl.BlockSpec((rows, GROUP_W), lambda i: (i, 0))
    vec = pl.BlockSpec((None, 1, D_MODEL), lambda i: (layer, 0, 0))
    return pl.pallas_call(
        _out_proj_kernel,
        grid=(m // rows,),
        in_specs=[part, part, part, part,
                  pl.BlockSpec((rows, D_MODEL), lambda i: (i, 0)),
                  pl.BlockSpec((None, D_MODEL, D_MODEL), lambda i: (layer, 0, 0)), vec, vec],
        out_specs=pl.BlockSpec((rows, D_MODEL), lambda i: (i, 0)),
        out_shape=jax.ShapeDtypeStruct((m, D_MODEL), F32),
        compiler_params=pltpu.CompilerParams(dimension_semantics=("parallel",), vmem_limit_bytes=VMEM_LIMIT),
        name="out_proj_ln",
    )(ya, yb, yc, yd, x, w_o, g, b)


def _ffn_kernel(tiles_per_seq, x_ref, xp_ref, wu_ref, cw_ref, cb_ref, wd_ref, g_ref, b_ref, o_ref,
                xb_ref, ha_ref, hb_ref, acc_ref):
    m = pl.program_id(0)
    cols = FFN_COLS
    n_col = D_FF // cols
    prev = jnp.where(m % tiles_per_seq == 0, 0.0, xp_ref[...])
    xb_ref[0:FFN_HALO, :] = prev.astype(BF16)
    xb_ref[FFN_HALO:, :] = x_ref[...].astype(BF16)

    def both(n, ref):
        return jnp.concatenate([ref[:, n * cols:(n + 1) * cols], ref[:, D_FF + n * cols:D_FF + (n + 1) * cols]],
                               axis=1)

    def up(n, h_ref):
        xb = xb_ref[...]
        h_ref[:, 0:cols] = jnp.dot(xb, wu_ref[:, n * cols:(n + 1) * cols], preferred_element_type=F32)
        h_ref[:, cols:2 * cols] = jnp.dot(xb, wu_ref[:, D_FF + n * cols:D_FF + (n + 1) * cols],
                                          preferred_element_type=F32)

    def gated(n, h_ref):
        h = h_ref[...]
        cw = both(n, cw_ref)
        c = (pltpu.roll(h, 2, 0) * cw[0:1, :] + pltpu.roll(h, 1, 0) * cw[1:2, :] + h * cw[2:3, :]
             + both(n, cb_ref))[FFN_HALO:, :]
        gate, val = c[:, 0:cols], c[:, cols:2 * cols]
        return (gate * jax.nn.sigmoid(gate) * val).astype(BF16)

    def down(n, act):
        return jnp.dot(act, wd_ref[n * cols:(n + 1) * cols, :], preferred_element_type=F32)

    bufs = (ha_ref, hb_ref)
    up(0, bufs[0])
    parts = []
    for n in range(n_col):
        if n + 1 < n_col:
            up(n + 1, bufs[(n + 1) % 2])
        parts.append(down(n, gated(n, bufs[n % 2])))
        if len(parts) == 2:
            if n == 1:
                acc_ref[...] = parts[0] + parts[1]
            else:
                acc_ref[...] += parts[0] + parts[1]
            parts = []
    y = acc_ref[...] + parts[0]
    o_ref[...] = _layer_norm_rows(DN_ALPHA * x_ref[...] + y, g_ref[...], b_ref[...])


def _conv_ffn_ln(x, seq, layer, w_up, conv_w, conv_b, w_down, g, b):
    m = x.shape[0]
    rows = FFN_ROWS
    n_col = D_FF // FFN_COLS
    assert m % rows == 0 and seq % rows == 0 and rows % FFN_HALO == 0 and n_col % 2 == 1 and n_col >= 3
    halo_blocks = rows // FFN_HALO
    resident = lambda *shape: pl.BlockSpec((None,) + shape, lambda i: (layer,) + (0,) * len(shape),
                                           pipeline_mode=pl.Buffered(1))
    return pl.pallas_call(
        functools.partial(_ffn_kernel, seq // rows),
        grid=(m // rows,),
        in_specs=[
            pl.BlockSpec((rows, D_MODEL), lambda i: (i, 0)),
            pl.BlockSpec((FFN_HALO, D_MODEL), lambda i: (jnp.maximum(i * halo_blocks - 1, 0), 0)),
            resident(D_MODEL, 2 * D_FF),
            resident(3, 2 * D_FF),
            resident(1, 2 * D_FF),
            resident(D_FF, D_MODEL),
            resident(1, D_MODEL), resident(1, D_MODEL),
        ],
        out_specs=pl.BlockSpec((rows, D_MODEL), lambda i: (i, 0)),
        out_shape=jax.ShapeDtypeStruct((m, D_MODEL), F32),
        scratch_shapes=[pltpu.VMEM((FFN_HALO + rows, D_MODEL), BF16),
                        pltpu.VMEM((FFN_HALO + rows, 2 * FFN_COLS), F32),
                        pltpu.VMEM((FFN_HALO + rows, 2 * FFN_COLS), F32),
                        pltpu.VMEM((rows, D_MODEL), F32)],
        compiler_params=pltpu.CompilerParams(dimension_semantics=("parallel",), vmem_limit_bytes=VMEM_LIMIT),
        name="conv_ffn_ln",
    )(x, x, w_up, conv_w, conv_b, w_down, g, b)


def _rotary_lane_tables(seq):
    pos = jnp.arange(seq, dtype=F32)
    inv_freq = ROPE_THETA ** (-jnp.arange(0, ROPE_DIM, 2, dtype=F32) / ROPE_DIM)
    half = ROPE_DIM // 2
    d = np.arange(PAIR_W) % HEAD_DIM
    freq_lane = jnp.where(d < ROPE_DIM, inv_freq[d % half], 0.0)
    ang = pos[:, None] * freq_lane[None, :]
    cos, sin = jnp.cos(ang), jnp.sin(ang)
    sin_a = jnp.where(d < half, -sin, 0.0)
    sin_b = jnp.where((d >= half) & (d < ROPE_DIM), sin, 0.0)
    return jnp.stack([cos, sin_a, sin_b])


def kernel(x, w_in, b_forget, sgu_ln_g, sgu_ln_b, sgu_w, sgu_b, pool_w, pool_scale, w_o, ln1_g, ln1_b,
           w_up, conv_w, conv_b, w_down, ln2_g, ln2_b):
    bsz, seq, _ = x.shape
    depth = w_in.shape[0]
    row = lambda t: t.reshape(depth, 1, t.shape[-1])
    rot = _rotary_lane_tables(seq)
    group = jnp.arange(GROUP_W) // HEAD_DIM
    avg = jnp.where(group[:, None] == group[None, :], 1.0 / HEAD_DIM, 0.0).astype(BF16)
    f0 = COL_FORGET
    w_rest = jnp.concatenate([w_in[:, :, f0 + N_HEADS:], w_in[:, :, f0:f0 + N_HEADS],
                              jnp.zeros((depth, D_MODEL, 128 - N_HEADS), F32)], axis=-1).astype(BF16)
    bfg = jnp.broadcast_to(jnp.pad(b_forget, ((0, 0), (0, 8 - N_HEADS)))[:, :, None], (depth, 8, 128))
    sgu_bias = jnp.repeat(sgu_b.transpose(0, 2, 1), HEAD_DIM, axis=2)
    n_pool = len(POOL_WINDOWS)
    same_group = jnp.eye(n_pool, dtype=bool)[None, :, None, :, None]
    pool_bd = jnp.where(same_group, pool_w[:, :, :, None, :], 0.0).reshape(
        depth, GROUP_W, GROUP_W).astype(BF16)
    w_o_b, w_up_b, w_down_b = w_o.astype(BF16), w_up.astype(BF16), w_down.astype(BF16)

    xf = x.reshape(bsz * seq, D_MODEL)
    for l in range(depth):
        qa, ka, va, qb, kb, vb, yc, yd, first = _mixer_front(
            xf.reshape(bsz, seq, D_MODEL), l, w_in, w_rest, rot, bfg, row(sgu_ln_g), row(sgu_ln_b), sgu_w,
            sgu_bias, avg, pool_bd, row(pool_scale))
        first_b = first[:, :, 0, :N_HEADS].reshape(-1)
        ya = _causal_attn(qa, ka, va, jnp.zeros_like(first_b)).reshape(bsz * seq, GROUP_W)
        yb = _causal_attn(qb, kb, vb, first_b).reshape(bsz * seq, GROUP_W)
        xf = _out_proj_ln(ya, yb, yc.reshape(bsz * seq, GROUP_W), yd.reshape(bsz * seq, GROUP_W), xf, l,
                          w_o_b, row(ln1_g), row(ln1_b))
        xf = _conv_ffn_ln(xf, seq, l, w_up_b, conv_w, row(conv_b), w_down_b, row(ln2_g), row(ln2_b))
    return xf.reshape(bsz, seq, D_MODEL)
```
